```python
import jax, jax.numpy as jnp
from jax import lax
import numpy as np

D_MODEL = 1024
BATCH = 8
SEQ = 2048
DEPTH = 1
DEC_BATCH = 128
DEC_SEQ = 1
PAST_LEN = 16384
PAGE_SIZE = 128

D_CONV = D_MODEL // 2
CONF_KERNEL = 31
N_HEADS = 8
HEAD_K = 128
HEAD_V = 128
QK_DIM = N_HEADS * HEAD_K
V_DIM = N_HEADS * HEAD_V
QKV_DIM = 2 * QK_DIM + V_DIM
SHORT_CONV = 4
CHUNK = 64
D_FF = 4 * D_MODEL
N_MOD = 6
EPS = 1e-6
IN_SIZES = (2 * D_CONV, QKV_DIM, V_DIM, N_HEADS, N_HEADS, D_MODEL, D_MODEL)
IN_OFFSETS = tuple(int(o) for o in np.cumsum(IN_SIZES)[:-1])
D_IN = int(sum(IN_SIZES))

kernel_name = 'hybrid_conformer_gdn_step'


def rms_norm(x, w):
    xf = x.astype(jnp.float32)
    y = xf * lax.rsqrt(jnp.mean(xf * xf, axis=-1, keepdims=True) + EPS)
    return (y * w.astype(jnp.float32)).astype(x.dtype)


def layer_norm(x, w, b):
    xf = x.astype(jnp.float32)
    mu = jnp.mean(xf, axis=-1, keepdims=True)
    xc = xf - mu
    var = jnp.mean(xc * xc, axis=-1, keepdims=True)
    y = xc * lax.rsqrt(var + EPS) * w.astype(jnp.float32) + b.astype(jnp.float32)
    return y.astype(x.dtype)


def l2norm(x):
    return x * lax.rsqrt(jnp.sum(x * x, axis=-1, keepdims=True) + EPS)


def causal_depthwise_conv(x_ext, w):
    c = x_ext.shape[-1]
    return lax.conv_general_dilated(
        x_ext, w.astype(x_ext.dtype)[:, None, :], window_strides=(1,), padding='VALID',
        dimension_numbers=('NWC', 'WIO', 'NWC'), feature_group_count=c)


def gated_delta_rule(q, k, v, g, beta, s0):
    bsz, seqlen, nh, dk = q.shape
    dv = v.shape[-1]
    f32 = jnp.float32
    q = l2norm(q.astype(f32)) * (dk ** -0.5)
    k = l2norm(k.astype(f32))
    v = v.astype(f32)
    g = g.astype(f32)
    beta = beta.astype(f32)
    csz = min(CHUNK, seqlen)
    pad = (-seqlen) % csz
    if pad:
        padf = lambda t: jnp.pad(t, [(0, 0), (0, pad)] + [(0, 0)] * (t.ndim - 2))
        q, k, v, g, beta = padf(q), padf(k), padf(v), padf(g), padf(beta)
    lp = seqlen + pad
    nc = lp // csz
    to_c = lambda t: t.reshape(bsz, nc, csz, nh, t.shape[-1]).transpose(0, 3, 1, 2, 4)
    q, k, v = to_c(q), to_c(k), to_c(v)
    g = g.reshape(bsz, nc, csz, nh).transpose(0, 3, 1, 2)
    beta = beta.reshape(bsz, nc, csz, nh).transpose(0, 3, 1, 2)
    g = jnp.cumsum(g, axis=-1)
    tri_incl = jnp.tril(jnp.ones((csz, csz), dtype=bool))
    tri_strict = jnp.tril(jnp.ones((csz, csz), dtype=bool), -1)
    decay = jnp.exp(jnp.where(tri_incl, g[..., :, None] - g[..., None, :], -jnp.inf))
    k_beta = k * beta[..., None]
    v_beta = v * beta[..., None]
    lower = jnp.where(tri_strict, jnp.einsum('bhncd,bhnsd->bhncs', k_beta, k) * decay, 0.0)
    eye = jnp.eye(csz, dtype=f32)
    t_inv = lax.linalg.triangular_solve(eye + lower, jnp.broadcast_to(eye, lower.shape),
                                        left_side=True, lower=True, unit_diagonal=True)
    value = jnp.einsum('bhncs,bhnsv->bhncv', t_inv, v_beta)
    k_cumdecay = jnp.einsum('bhncs,bhnsd->bhncd', t_inv, k_beta * jnp.exp(g)[..., None])
    attn_intra = jnp.einsum('bhncd,bhnsd->bhncs', q, k) * decay

    def step(s, inp):
        q_c, k_c, val_c, kcd_c, att_c, g_c = inp
        v_new = val_c - jnp.einsum('bhck,bhkv->bhcv', kcd_c, s)
        o_c = (jnp.einsum('bhck,bhkv->bhcv', q_c * jnp.exp(g_c)[..., None], s)
               + jnp.einsum('bhcs,bhsv->bhcv', att_c, v_new))
        g_last = g_c[..., -1]
        s = (s * jnp.exp(g_last)[..., None, None]
             + jnp.einsum('bhck,bhcv->bhkv', k_c * jnp.exp(g_last[..., None] - g_c)[..., None], v_new))
        return s, o_c

    xs = tuple(jnp.moveaxis(t, 2, 0) for t in (q, k, value, k_cumdecay, attn_intra, g))
    s_fin, o = lax.scan(step, s0.astype(f32), xs)
    o = o.transpose(1, 0, 3, 2, 4).reshape(bsz, lp, nh, dv)[:, :seqlen]
    return o, s_fin


def hybrid_layer(x, c, conf_buf, qkv_buf, s0, w_ada, b_ada, norm1_w, w_in, conf_dw_w, conf_dw_b,
                 conf_ln_w, conf_ln_b, w_conf_out, gdn_conv_w, a_log, dt_bias, gdn_norm_w,
                 w_gdn_out, w_o, norm2_w, w_ff1, w_ff2):
    bsz, seqlen, _ = x.shape
    f32 = jnp.float32
    mod = jnp.einsum('bd,de->be', jax.nn.silu(c), w_ada) + b_ada
    shift1, scale1, gate1, shift2, scale2, gate2 = [m[:, None, :] for m in jnp.split(mod, N_MOD, axis=-1)]
    h = rms_norm(x, norm1_w) * (1 + scale1) + shift1
    proj = jnp.einsum('bld,de->ble', h, w_in)
    u_glu, qkv_raw, z, b_raw, a_raw, gate_a, gate_b = jnp.split(proj, IN_OFFSETS, axis=-1)

    glu = u_glu[..., :D_CONV] * jax.nn.sigmoid(u_glu[..., D_CONV:])
    glu_ext = jnp.concatenate([conf_buf.astype(glu.dtype), glu], axis=1)
    new_conf = glu_ext[:, glu_ext.shape[1] - (CONF_KERNEL - 1):]
    a = causal_depthwise_conv(glu_ext, conf_dw_w) + conf_dw_b
    a = jax.nn.silu(layer_norm(a, conf_ln_w, conf_ln_b))
    y_a = jnp.einsum('blc,cd->bld', a, w_conf_out)

    qkv_ext = jnp.concatenate([qkv_buf.astype(qkv_raw.dtype), qkv_raw], axis=1)
    new_qkv = qkv_ext[:, qkv_ext.shape[1] - (SHORT_CONV - 1):]
    qkv = jax.nn.silu(causal_depthwise_conv(qkv_ext, gdn_conv_w))
    q, k, v = jnp.split(qkv, (QK_DIM, 2 * QK_DIM), axis=-1)
    q = q.reshape(bsz, seqlen, N_HEADS, HEAD_K)
    k = k.reshape(bsz, seqlen, N_HEADS, HEAD_K)
    v = v.reshape(bsz, seqlen, N_HEADS, HEAD_V)
    beta = jax.nn.sigmoid(b_raw.astype(f32))
    g = -jnp.exp(a_log.astype(f32)) * jax.nn.softplus(a_raw.astype(f32) + dt_bias.astype(f32))
    o, s_new = gated_delta_rule(q, k, v, g, beta, s0)
    o = rms_norm(o.astype(x.dtype), gdn_norm_w) * jax.nn.silu(z.reshape(bsz, seqlen, N_HEADS, HEAD_V))
    y_b = jnp.einsum('blhv,hvd->bld', o, w_gdn_out.reshape(N_HEADS, HEAD_V, D_MODEL))

    merged = jax.nn.sigmoid(gate_a) * y_a + jax.nn.sigmoid(gate_b) * y_b
    x = x + gate1 * jnp.einsum('bld,de->ble', merged, w_o)

    h2 = rms_norm(x, norm2_w) * (1 + scale2) + shift2
    f = jnp.square(jax.nn.relu(jnp.einsum('bld,df->blf', h2, w_ff1)))
    x = x + gate2 * jnp.einsum('blf,fd->bld', f, w_ff2)
    return x, new_conf, new_qkv, s_new.astype(x.dtype)


def _normal(k, shape, scale):
    return jax.random.normal(k, shape, jnp.float32) * scale


def setup_inputs(seed: int = 0) -> dict:
    key = jax.random.key(seed)
    ks = jax.random.split(key, 32)
    dt = jnp.exp(jax.random.uniform(ks[20], (DEPTH, N_HEADS), jnp.float32, np.log(1e-3), np.log(1e-1)))
    return {
        'x_prompt': _normal(ks[0], (BATCH, SEQ, D_MODEL), 1.0),
        'x_sample': _normal(ks[1], (DEC_BATCH, DEC_SEQ, D_MODEL), 1.0),
        'c_prompt': _normal(ks[2], (BATCH, D_MODEL), 1.0),
        'c_sample': _normal(ks[3], (DEC_BATCH, D_MODEL), 1.0),
        'state_conf_conv': _normal(ks[4], (DEPTH, DEC_BATCH, CONF_KERNEL - 1, D_CONV), 0.5),
        'state_qkv_conv': _normal(ks[5], (DEPTH, DEC_BATCH, SHORT_CONV - 1, QKV_DIM), 1.0),
        'state_delta': _normal(ks[6], (DEPTH, DEC_BATCH, N_HEADS, HEAD_K, HEAD_V), 0.1),
        'w_ada': _normal(ks[7], (DEPTH, D_MODEL, N_MOD * D_MODEL), D_MODEL ** -0.5),
        'b_ada': _normal(ks[8], (DEPTH, N_MOD * D_MODEL), 0.02),
        'norm1_w': 1.0 + _normal(ks[9], (DEPTH, D_MODEL), 0.02),
        'w_in': _normal(ks[10], (DEPTH, D_MODEL, D_IN), D_MODEL ** -0.5),
        'conf_dw_w': _normal(ks[11], (DEPTH, CONF_KERNEL, D_CONV), CONF_KERNEL ** -0.5),
        'conf_dw_b': _normal(ks[12], (DEPTH, D_CONV), 0.02),
        'conf_ln_w': 1.0 + _normal(ks[13], (DEPTH, D_CONV), 0.02),
        'conf_ln_b': _normal(ks[14], (DEPTH, D_CONV), 0.02),
        'w_conf_out': _normal(ks[15], (DEPTH, D_CONV, D_MODEL), D_CONV ** -0.5),
        'gdn_conv_w': _normal(ks[16], (DEPTH, SHORT_CONV, QKV_DIM), SHORT_CONV ** -0.5),
        'a_log': jnp.log(jax.random.uniform(ks[17], (DEPTH, N_HEADS), jnp.float32, 1.0, 16.0)),
        'dt_bias': dt + jnp.log(-jnp.expm1(-dt)),
        'gdn_norm_w': 1.0 + _normal(ks[18], (DEPTH, HEAD_V), 0.02),
        'w_gdn_out': _normal(ks[19], (DEPTH, V_DIM, D_MODEL), V_DIM ** -0.5),
        'w_o': _normal(ks[21], (DEPTH, D_MODEL, D_MODEL), D_MODEL ** -0.5),
        'norm2_w': 1.0 + _normal(ks[22], (DEPTH, D_MODEL), 0.02),
        'w_ff1': _normal(ks[23], (DEPTH, D_MODEL, D_FF), D_MODEL ** -0.5),
        'w_ff2': _normal(ks[24], (DEPTH, D_FF, D_MODEL), D_FF ** -0.5),
        'final_norm_w': 1.0 + _normal(ks[25], (D_MODEL,), 0.02),
    }


def reference(x_prompt, x_sample, c_prompt, c_sample, state_conf_conv, state_qkv_conv, state_delta,
              w_ada, b_ada, norm1_w, w_in, conf_dw_w, conf_dw_b, conf_ln_w, conf_ln_b, w_conf_out,
              gdn_conv_w, a_log, dt_bias, gdn_norm_w, w_gdn_out, w_o, norm2_w, w_ff1, w_ff2,
              final_norm_w):
    bp = x_prompt.shape[0]
    hp, hs = x_prompt, x_sample
    conf_p, qkv_p, delta_p = [], [], []
    conf_s, qkv_s, delta_s = [], [], []
    for l in range(DEPTH):
        lw = (w_ada[l], b_ada[l], norm1_w[l], w_in[l], conf_dw_w[l], conf_dw_b[l], conf_ln_w[l],
              conf_ln_b[l], w_conf_out[l], gdn_conv_w[l], a_log[l], dt_bias[l], gdn_norm_w[l],
              w_gdn_out[l], w_o[l], norm2_w[l], w_ff1[l], w_ff2[l])
        conf0 = jnp.zeros((bp, CONF_KERNEL - 1, D_CONV), x_prompt.dtype)
        qkv0 = jnp.zeros((bp, SHORT_CONV - 1, QKV_DIM), x_prompt.dtype)
        s00 = jnp.zeros((bp, N_HEADS, HEAD_K, HEAD_V), jnp.float32)
        hp, cp, qp, sp = hybrid_layer(hp, c_prompt, conf0, qkv0, s00, *lw)
        hs, cs, qs, ss = hybrid_layer(hs, c_sample, state_conf_conv[l], state_qkv_conv[l], state_delta[l], *lw)
        conf_p.append(cp); qkv_p.append(qp); delta_p.append(sp)
        conf_s.append(cs); qkv_s.append(qs); delta_s.append(ss)
    y_prompt = rms_norm(hp, final_norm_w)
    y_sample = rms_norm(hs, final_norm_w)
    return (y_prompt, y_sample, jnp.stack(conf_p), jnp.stack(qkv_p), jnp.stack(delta_p),
            jnp.stack(conf_s), jnp.stack(qkv_s), jnp.stack(delta_s))
```

```python
import functools

import jax
import jax.numpy as jnp
from jax import lax
from jax.experimental import pallas as pl
from jax.experimental.pallas import tpu as pltpu

F32 = jnp.float32
BF16 = jnp.bfloat16

EPS = 1e-6
N_MOD = 6
N_HEADS = 8
HEAD_DIM = 128
CHUNK = 64
LANES = 128
SUBLANES = 8
VMEM_LIMIT_BYTES = 56 * 1024 * 1024

PRE_TILE = 256
POST_TILE = 256
GDN_WINDOW = 128
CONV_HALO = 32
SAMPLE_SEQ_BLOCK = 8


def _sigmoid(x):
    return 1.0 / (1.0 + jnp.exp(-x))


def _silu(x):
    return x * _sigmoid(x)


def _softplus(x):
    return jnp.maximum(x, 0.0) + jnp.log(1.0 + jnp.exp(-jnp.abs(x)))


def _dot(a, b):
    return jnp.dot(a, b, preferred_element_type=F32)


def _dot_nt(a, b):
    return lax.dot_general(a, b, (((1,), (1,)), ((), ())), preferred_element_type=F32)


def _resident(shape):
    n = len(shape)
    return pl.BlockSpec(shape, lambda *_: (0,) * n, pipeline_mode=pl.Buffered(1))


def _params(n_grid):
    return pltpu.CompilerParams(dimension_semantics=("arbitrary",) * n_grid,
                                vmem_limit_bytes=VMEM_LIMIT_BYTES)


def _mod_kernel(c_ref, w_ref, b_ref, o_ref):
    c = c_ref[...]
    o_ref[...] = _dot(_silu(c).astype(BF16), w_ref[...]) + b_ref[...]


def _mod_call(c_all, w_ada, b_ada):
    n, d = c_all.shape
    e = w_ada.shape[1]
    tn = e // 4
    return pl.pallas_call(
        _mod_kernel,
        grid=(e // tn,),
        in_specs=[pl.BlockSpec((n, d), lambda j: (0, 0)),
                  pl.BlockSpec((d, tn), lambda j: (0, j)),
                  pl.BlockSpec((1, tn), lambda j: (0, j))],
        out_specs=pl.BlockSpec((n, tn), lambda j: (0, j)),
        out_shape=jax.ShapeDtypeStruct((n, e), F32),
        compiler_params=_params(1),
        name="mod",
    )(c_all, w_ada, b_ada)


def _pre_front(x, shift1, scale1, n1w):
    ms = jnp.mean(x * x, axis=-1, keepdims=True)
    h = x * lax.rsqrt(ms + EPS) * n1w
    return (h * (1.0 + scale1) + shift1).astype(BF16)


def _qkv_finish(y, g, q_ref, k_ref, v_ref):
    s = _silu(y)
    d = q_ref.shape[1]
    n_qk = d // LANES
    if g < 2 * n_qk:
        s = s * lax.rsqrt(jnp.sum(s * s, axis=-1, keepdims=True) + EPS)
    if g < n_qk:
        q_ref[:, g * LANES:(g + 1) * LANES] = s * (HEAD_DIM ** -0.5)
    elif g < 2 * n_qk:
        k_ref[:, (g - n_qk) * LANES:(g - n_qk + 1) * LANES] = s
    else:
        v_ref[:, (g - 2 * n_qk) * LANES:(g - 2 * n_qk + 1) * LANES] = s


def _beta_decay(ba, alog, dtb):
    lane = lax.broadcasted_iota(jnp.int32, ba.shape, 1)
    beta = _sigmoid(ba)
    g = -jnp.exp(alog) * _softplus(ba + dtb)
    return jnp.where(lane < N_HEADS, beta, g)


def _pre_prompt_kernel(x_ref, mod_ref, n1w_ref, w_ref, cw_ref, alog_ref, dtb_ref, cum_ref,
                       glu_ref, q_ref, k_ref, v_ref, z_ref, ga_ref, gb_ref, bg_ref, tail_ref,
                       qkv_scr, *, tm, tiles_per_seq, d, dc, dqkv):
    i = pl.program_id(0)
    hb = _pre_front(x_ref[...], mod_ref[0, :, 0:d], mod_ref[0, :, d:2 * d], n1w_ref[...])

    u = _dot(hb, w_ref[:, 0:2 * dc])
    glu_ref[...] = u[:, :dc] * _sigmoid(u[:, dc:])

    @pl.when(i % tiles_per_seq == 0)
    def _():
        qkv_scr[0:SUBLANES, :] = jnp.zeros((SUBLANES, dqkv), F32)

    o_qkv = 2 * dc
    step = 512
    for c in range(dqkv // step):
        qkv_scr[SUBLANES:SUBLANES + tm, c * step:(c + 1) * step] = _dot(
            hb, w_ref[:, o_qkv + c * step:o_qkv + (c + 1) * step])
    tail_ref[...] = qkv_scr[tm:tm + SUBLANES, :]

    n_taps = cw_ref.shape[0]
    for g in range(dqkv // LANES):
        cols = slice(g * LANES, (g + 1) * LANES)
        y = None
        for j in range(n_taps):
            r0 = SUBLANES - (n_taps - 1) + j
            term = cw_ref[j:j + 1, cols] * qkv_scr[r0:r0 + tm, cols]
            y = term if y is None else y + term
        _qkv_finish(y, g, q_ref, k_ref, v_ref)
    qkv_scr[0:SUBLANES, :] = qkv_scr[tm:tm + SUBLANES, :]

    o_z = o_qkv + dqkv
    z_ref[...] = _dot(hb, w_ref[:, o_z:o_z + d]).astype(BF16)
    ga_ref[...] = _sigmoid(_dot(hb, w_ref[:, o_z + d:o_z + 2 * d])).astype(BF16)
    gb_ref[...] = _sigmoid(_dot(hb, w_ref[:, o_z + 2 * d:o_z + 3 * d])).astype(BF16)

    ba = _dot(hb, w_ref[:, o_z + 3 * d:o_z + 3 * d + LANES])
    bg = _beta_decay(ba, alog_ref[...], dtb_ref[...])
    cum = jnp.dot(cum_ref[...], bg, preferred_element_type=F32, precision=lax.Precision.HIGHEST)
    lane = lax.broadcasted_iota(jnp.int32, bg.shape, 1)
    bg_ref[...] = jnp.where(lane < N_HEADS, bg, cum)


def _pre_prompt_call(x, mod_p, n1w, w_pack, cw, alog_p, dtb_p, seq_len):
    t, d = x.shape
    tm = PRE_TILE
    assert seq_len % tm == 0 and tm % CHUNK == 0
    dqkv = cw.shape[1]
    dc = (w_pack.shape[1] - dqkv - 3 * d - LANES) // 2
    nb = t // seq_len
    tiles_per_seq = seq_len // tm
    row = lax.broadcasted_iota(jnp.int32, (tm, tm), 0)
    col = lax.broadcasted_iota(jnp.int32, (tm, tm), 1)
    cum_mat = ((row // CHUNK == col // CHUNK) & (col <= row)).astype(F32)

    tok = lambda w: pl.BlockSpec((tm, w), lambda i: (i, 0))
    kern = functools.partial(_pre_prompt_kernel, tm=tm, tiles_per_seq=tiles_per_seq, d=d, dc=dc, dqkv=dqkv)
    return pl.pallas_call(
        kern,
        grid=(t // tm,),
        in_specs=[tok(d),
                  pl.BlockSpec((1, 1, mod_p.shape[2]), lambda i: (i // tiles_per_seq, 0, 0)),
                  _resident(n1w.shape), _resident(w_pack.shape), _resident(cw.shape),
                  _resident(alog_p.shape), _resident(dtb_p.shape), _resident(cum_mat.shape)],
        out_specs=[tok(dc), tok(d), tok(d), tok(d), tok(d), tok(d), tok(d), tok(LANES),
                   pl.BlockSpec((SUBLANES, dqkv), lambda i: (i // tiles_per_seq, 0))],
        out_shape=[jax.ShapeDtypeStruct((t, dc), F32),
                   jax.ShapeDtypeStruct((t, d), F32), jax.ShapeDtypeStruct((t, d), F32),
                   jax.ShapeDtypeStruct((t, d), F32), jax.ShapeDtypeStruct((t, d), BF16),
                   jax.ShapeDtypeStruct((t, d), BF16), jax.ShapeDtypeStruct((t, d), BF16),
                   jax.ShapeDtypeStruct((t, LANES), F32),
                   jax.ShapeDtypeStruct((nb * SUBLANES, dqkv), F32)],
        scratch_shapes=[pltpu.VMEM((tm + SUBLANES, dqkv), F32)],
        compiler_params=_params(1),
        name="pre_prompt",
    )(x, mod_p, n1w, w_pack, cw, alog_p, dtb_p, cum_mat)


def _pre_sample_kernel(x_ref, mod_ref, n1w_ref, w_ref, cw_ref, alog_ref, dtb_ref,
                       sconf_ref, sqkv_ref, dww_ref, dwb_ref,
                       apre_ref, q_ref, k_ref, v_ref, z_ref, ga_ref, gb_ref, bg_ref,
                       nconf_ref, nqkv_ref, *, d, dc, dqkv):
    hb = _pre_front(x_ref[...], mod_ref[:, 0:d], mod_ref[:, d:2 * d], n1w_ref[...])

    u = _dot(hb, w_ref[:, 0:2 * dc])
    glu = u[:, :dc] * _sigmoid(u[:, dc:])
    n_hist = sconf_ref.shape[1] // dc
    acc = dww_ref[n_hist:n_hist + 1, :] * glu + dwb_ref[...]
    for j in range(n_hist):
        acc = acc + dww_ref[j:j + 1, :] * sconf_ref[:, j * dc:(j + 1) * dc]
    apre_ref[...] = acc
    nconf_ref[:, 0:(n_hist - 1) * dc] = sconf_ref[:, dc:n_hist * dc]
    nconf_ref[:, (n_hist - 1) * dc:n_hist * dc] = glu

    o_qkv = 2 * dc
    n_taps = cw_ref.shape[0]
    nqkv_ref[:, 0:(n_taps - 2) * dqkv] = sqkv_ref[:, dqkv:(n_taps - 1) * dqkv]
    for g in range(dqkv // LANES):
        cols = slice(g * LANES, (g + 1) * LANES)
        raw = _dot(hb, w_ref[:, o_qkv + g * LANES:o_qkv + (g + 1) * LANES])
        nqkv_ref[:, (n_taps - 2) * dqkv + g * LANES:(n_taps - 2) * dqkv + (g + 1) * LANES] = raw
        y = cw_ref[n_taps - 1:n_taps, cols] * raw
        for j in range(n_taps - 1):
            y = y + cw_ref[j:j + 1, cols] * sqkv_ref[:, j * dqkv + g * LANES:j * dqkv + (g + 1) * LANES]
        _qkv_finish(y, g, q_ref, k_ref, v_ref)

    o_z = o_qkv + dqkv
    z_ref[...] = _dot(hb, w_ref[:, o_z:o_z + d]).astype(BF16)
    ga_ref[...] = _sigmoid(_dot(hb, w_ref[:, o_z + d:o_z + 2 * d])).astype(BF16)
    gb_ref[...] = _sigmoid(_dot(hb, w_ref[:, o_z + 2 * d:o_z + 3 * d])).astype(BF16)
    ba = _dot(hb, w_ref[:, o_z + 3 * d:o_z + 3 * d + LANES])
    bg_ref[...] = _beta_decay(ba, alog_ref[...], dtb_ref[...])


def _pre_sample_call(x, mod_s, n1w, w_pack, cw, alog_p, dtb_p, sconf, sqkv, dww, dwb):
    n, d = x.shape
    dqkv = cw.shape[1]
    dc = dwb.shape[1]
    tm = 32
    assert n % tm == 0
    tok = lambda w: pl.BlockSpec((tm, w), lambda i: (i, 0))
    kern = functools.partial(_pre_sample_kernel, d=d, dc=dc, dqkv=dqkv)
    return pl.pallas_call(
        kern,
        grid=(n // tm,),
        in_specs=[tok(d), tok(mod_s.shape[1]),
                  _resident(n1w.shape), _resident(w_pack.shape), _resident(cw.shape),
                  _resident(alog_p.shape), _resident(dtb_p.shape),
                  tok(sconf.shape[1]), tok(sqkv.shape[1]),
                  _resident(dww.shape), _resident(dwb.shape)],
        out_specs=[tok(dc), tok(d), tok(d), tok(d), tok(d), tok(d), tok(d), tok(LANES),
                   tok(sconf.shape[1]), tok(sqkv.shape[1])],
        out_shape=[jax.ShapeDtypeStruct((n, dc), F32),
                   jax.ShapeDtypeStruct((n, d), F32), jax.ShapeDtypeStruct((n, d), F32),
                   jax.ShapeDtypeStruct((n, d), F32), jax.ShapeDtypeStruct((n, d), BF16),
                   jax.ShapeDtypeStruct((n, d), BF16), jax.ShapeDtypeStruct((n, d), BF16),
                   jax.ShapeDtypeStruct((n, LANES), F32),
                   jax.ShapeDtypeStruct(sconf.shape, F32), jax.ShapeDtypeStruct(sqkv.shape, F32)],
        compiler_params=_params(1),
        name="pre_sample",
    )(x, mod_s, n1w, w_pack, cw, alog_p, dtb_p, sconf, sqkv, dww, dwb)


def _half_block_mask(row, col, size):
    shift = size.bit_length() - 1
    half = size // 2
    return ((lax.shift_right_logical(row, shift) == lax.shift_right_logical(col, shift))
            & ((row & half) != 0) & ((col & half) == 0))


def _gdn_prompt_kernel(q_ref, k_ref, v_ref, bg_ref, bgt_ref, o_ref, sfin_ref, s_scr, *, w, n_win):
    wi = pl.program_id(1)

    @pl.when(wi == 0)
    def _():
        s_scr[...] = jnp.zeros(s_scr.shape, F32)

    nc = w // CHUNK
    row = lax.broadcasted_iota(jnp.int32, (w, w), 0)
    col = lax.broadcasted_iota(jnp.int32, (w, w), 1)
    shift = CHUNK.bit_length() - 1
    same = lax.shift_right_logical(row, shift) == lax.shift_right_logical(col, shift)
    m_incl = same & (col <= row)
    m_strict = same & (col < row)
    eye = (row == col).astype(F32)

    for h in range(N_HEADS):
        hs = slice(h * HEAD_DIM, (h + 1) * HEAD_DIM)
        qh = q_ref[:, hs]
        kh = k_ref[:, hs]
        vh = v_ref[:, hs]
        bcol = jnp.broadcast_to(bg_ref[:, h:h + 1], (w, HEAD_DIM))
        gcb = jnp.broadcast_to(bg_ref[:, N_HEADS + h:N_HEADS + h + 1], (w, HEAD_DIM))
        grow = bgt_ref[N_HEADS + h:N_HEADS + h + 1, :]

        decay = jnp.exp(jnp.where(m_incl, gcb - grow, -1e30))
        kb = kh.astype(BF16)
        kk = _dot_nt(kb, kb)
        att = _dot_nt(qh.astype(BF16), kb) * decay
        amat = jnp.where(m_strict, kk * bcol * decay, 0.0)

        x = eye - jnp.where(_half_block_mask(row, col, 2), amat, 0.0)
        size = 4
        while size <= CHUNK:
            blk = jnp.where(_half_block_mask(row, col, size), amat, 0.0).astype(BF16)
            y = _dot(blk, x.astype(BF16))
            x = x - _dot(x.astype(BF16), y.astype(BF16))
            size *= 2
        tb = x.astype(BF16)

        eg = jnp.exp(gcb)
        rhs = jnp.concatenate([vh * bcol, kh * (bcol * eg)], axis=1).astype(BF16)
        vk = _dot(tb, rhs)
        value = vk[:, :HEAD_DIM]
        kcd = vk[:, HEAD_DIM:]
        qe = qh * eg

        glast = jnp.concatenate(
            [jnp.broadcast_to(gcb[(c + 1) * CHUNK - 1:(c + 1) * CHUNK, :], (CHUNK, HEAD_DIM)) for c in range(nc)],
            axis=0)
        ktt = (kh * jnp.exp(glast - gcb)).T.astype(BF16)

        s = s_scr[h]
        for c in range(nc):
            cs = slice(c * CHUNK, (c + 1) * CHUNK)
            lhs = jnp.concatenate([kcd[cs], qe[cs]], axis=0).astype(BF16)
            ps = _dot(lhs, s.astype(BF16))
            v_new = value[cs] - ps[:CHUNK]
            parts = [jnp.zeros((CHUNK, HEAD_DIM), F32)] * nc
            parts[c] = v_new
            vpad = jnp.concatenate(parts, axis=0).astype(BF16)
            o_ref[cs, hs] = ps[CHUNK:] + _dot(att[cs].astype(BF16), vpad)
            s = s * jnp.exp(glast[c * CHUNK:c * CHUNK + 1, :]) + _dot(ktt, vpad)
        s_scr[h] = s

    @pl.when(wi == n_win - 1)
    def _():
        sfin_ref[0] = s_scr[...]


def _gdn_prompt_call(qn, kn, v, bg, bgt, nb, seq_len):
    t, d = qn.shape
    w = GDN_WINDOW
    assert seq_len % w == 0 and w % CHUNK == 0 and w == HEAD_DIM
    n_win = seq_len // w
    tok = pl.BlockSpec((w, d), lambda b, i: (b * n_win + i, 0))
    kern = functools.partial(_gdn_prompt_kernel, w=w, n_win=n_win)
    return pl.pallas_call(
        kern,
        grid=(nb, n_win),
        in_specs=[tok, tok, tok,
                  pl.BlockSpec((w, LANES), lambda b, i: (b * n_win + i, 0)),
                  pl.BlockSpec((LANES, w), lambda b, i: (0, b * n_win + i))],
        out_specs=[tok, pl.BlockSpec((1, N_HEADS, HEAD_DIM, HEAD_DIM), lambda b, i: (b, 0, 0, 0))],
        out_shape=[jax.ShapeDtypeStruct((t, d), F32),
                   jax.ShapeDtypeStruct((nb, N_HEADS, HEAD_DIM, HEAD_DIM), F32)],
        scratch_shapes=[pltpu.VMEM((N_HEADS, HEAD_DIM, HEAD_DIM), F32)],
        compiler_params=_params(2),
        name="gdn_prompt",
    )(qn, kn, v, bg, bgt)


def _gdn_sample_kernel(q_ref, k_ref, v_ref, bg_ref, s_ref, o_ref, snew_ref, *, bs):
    row = lax.broadcasted_iota(jnp.int32, (HEAD_DIM, HEAD_DIM), 0)
    col = lax.broadcasted_iota(jnp.int32, (HEAD_DIM, HEAD_DIM), 1)
    eye = row == col
    for h in range(N_HEADS):
        hs = slice(h * HEAD_DIM, (h + 1) * HEAD_DIM)
        qh = q_ref[:, hs]
        kh = k_ref[:, hs]
        vh = v_ref[:, hs]
        beta = jnp.broadcast_to(bg_ref[:, h:h + 1], (bs, HEAD_DIM))
        eg = jnp.exp(jnp.broadcast_to(bg_ref[:, N_HEADS + h:N_HEADS + h + 1], (bs, HEAD_DIM)))
        att = jnp.sum(qh * kh, axis=-1, keepdims=True)
        for i in range(bs):
            s = s_ref[i, h]
            kcol = jnp.sum(jnp.where(eye, jnp.broadcast_to(kh[i:i + 1, :], (HEAD_DIM, HEAD_DIM)), 0.0),
                           axis=-1, keepdims=True)
            qcol = jnp.sum(jnp.where(eye, jnp.broadcast_to(qh[i:i + 1, :], (HEAD_DIM, HEAD_DIM)), 0.0),
                           axis=-1, keepdims=True)
            ks = jnp.sum(kcol * s, axis=0, keepdims=True)
            qs = jnp.sum(qcol * s, axis=0, keepdims=True)
            egi = eg[i:i + 1, :]
            v_new = beta[i:i + 1, :] * (vh[i:i + 1, :] - egi * ks)
            o_ref[i:i + 1, hs] = egi * qs + att[i:i + 1, :] * v_new
            snew_ref[i, h] = s * egi + kcol * v_new


def _gdn_sample_call(qn, kn, v, bg, s0):
    n, d = qn.shape
    bs = SAMPLE_SEQ_BLOCK
    assert n % bs == 0
    tok = lambda w: pl.BlockSpec((bs, w), lambda i: (i, 0))
    st = pl.BlockSpec((bs, N_HEADS, HEAD_DIM, HEAD_DIM), lambda i: (i, 0, 0, 0))
    return pl.pallas_call(
        functools.partial(_gdn_sample_kernel, bs=bs),
        grid=(n // bs,),
        in_specs=[tok(d), tok(d), tok(d), tok(LANES), st],
        out_specs=[tok(d), st],
        out_shape=[jax.ShapeDtypeStruct((n, d), F32), jax.ShapeDtypeStruct(s0.shape, F32)],
        compiler_params=_params(1),
        name="gdn_sample",
    )(qn, kn, v, bg, s0)


def _rms(x, w):
    return x * lax.rsqrt(jnp.mean(x * x, axis=-1, keepdims=True) + EPS) * w


def _post_kernel(*refs, tm, tiles_per_seq, d, dc, dff, conv_in_kernel, per_token_mod):
    if conv_in_kernel:
        (x_ref, mod_ref, glu_ref, halo_ref, dww_ref, dwb_ref, lnw_ref, lnb_ref,
         o_ref, z_ref, gnw_ref, ga_ref, gb_ref, wca_ref, wgo_ref, wo_ref, n2w_ref, w1_ref, w2_ref, fnw_ref,
         y_ref, ext_scr) = refs
    else:
        (x_ref, mod_ref, apre_ref, lnw_ref, lnb_ref,
         o_ref, z_ref, gnw_ref, ga_ref, gb_ref, wca_ref, wgo_ref, wo_ref, n2w_ref, w1_ref, w2_ref, fnw_ref,
         y_ref) = refs

    def mod(k):
        if per_token_mod:
            return mod_ref[:, k * d:(k + 1) * d]
        return mod_ref[0, :, k * d:(k + 1) * d]

    if conv_in_kernel:
        i = pl.program_id(0)
        halo = halo_ref[...]
        ext_scr[0:CONV_HALO, :] = jnp.where(i % tiles_per_seq == 0, 0.0, halo)
        ext_scr[CONV_HALO:CONV_HALO + tm, :] = glu_ref[...]
        n_taps = dww_ref.shape[0]
        a = dwb_ref[...] + jnp.zeros((tm, dc), F32)
        for j in range(n_taps):
            r0 = CONV_HALO - (n_taps - 1) + j
            a = a + dww_ref[j:j + 1, :] * ext_scr[r0:r0 + tm, :]
    else:
        a = apre_ref[...]
    mu = jnp.mean(a, axis=-1, keepdims=True)
    ac = a - mu
    var = jnp.mean(ac * ac, axis=-1, keepdims=True)
    a = _silu(ac * lax.rsqrt(var + EPS) * lnw_ref[...] + lnb_ref[...])
    y_a = _dot(a.astype(BF16), wca_ref[...])

    gated = []
    for h in range(N_HEADS):
        hs = slice(h * HEAD_DIM, (h + 1) * HEAD_DIM)
        gated.append(_rms(o_ref[:, hs], gnw_ref[...]) * _silu(z_ref[:, hs].astype(F32)))
    y_b = _dot(jnp.concatenate(gated, axis=1).astype(BF16), wgo_ref[...])

    merged = ga_ref[...].astype(F32) * y_a + gb_ref[...].astype(F32) * y_b
    x1 = x_ref[...] + mod(2) * _dot(merged.astype(BF16), wo_ref[...])

    h2 = (_rms(x1, n2w_ref[...]) * (1.0 + mod(4)) + mod(3)).astype(BF16)
    step = 1024
    ff = None
    for c in range(dff // step):
        f = jnp.maximum(_dot(h2, w1_ref[:, c * step:(c + 1) * step]), 0.0)
        part = _dot((f * f).astype(BF16), w2_ref[c * step:(c + 1) * step, :])
        ff = part if ff is None else ff + part
    x2 = x1 + mod(5) * ff
    y_ref[...] = _rms(x2, fnw_ref[...])


def _post_call(x, mod, conv_in, lnw, lnb, o, z, gnw, ga, gb, wca, wgo, wo, n2w, w1, w2, fnw,
               *, seq_len, per_token_mod):
    t, d = x.shape
    dc = lnw.shape[1]
    dff = w1.shape[1]
    conv_in_kernel = len(conv_in) == 3
    tm = min(POST_TILE, t)
    assert t % tm == 0 and seq_len % tm == 0 or not conv_in_kernel
    tiles_per_seq = max(seq_len // tm, 1)
    tok = lambda w: pl.BlockSpec((tm, w), lambda i: (i, 0))
    if per_token_mod:
        mod_spec = tok(mod.shape[1])
    else:
        mod_spec = pl.BlockSpec((1, 1, mod.shape[2]), lambda i: (i // tiles_per_seq, 0, 0))
    if conv_in_kernel:
        glu, dww, dwb = conv_in
        r = tm // CONV_HALO
        conv_args = [glu, glu, dww, dwb]
        conv_specs = [tok(dc),
                      pl.BlockSpec((CONV_HALO, dc), lambda i: (jnp.maximum(i * r - 1, 0), 0)),
                      _resident(dww.shape), _resident(dwb.shape)]
        scratch = [pltpu.VMEM((tm + CONV_HALO, dc), F32)]
    else:
        conv_args = [conv_in[0]]
        conv_specs = [tok(dc)]
        scratch = []
    args = [x, mod] + conv_args + [lnw, lnb, o, z, gnw, ga, gb, wca, wgo, wo, n2w, w1, w2, fnw]
    specs = ([tok(d), mod_spec] + conv_specs
             + [_resident(lnw.shape), _resident(lnb.shape), tok(d), tok(d), _resident(gnw.shape), tok(d), tok(d),
                _resident(wca.shape), _resident(wgo.shape), _resident(wo.shape), _resident(n2w.shape),
                _resident(w1.shape), _resident(w2.shape), _resident(fnw.shape)])
    kern = functools.partial(_post_kernel, tm=tm, tiles_per_seq=tiles_per_seq, d=d, dc=dc, dff=dff,
                             conv_in_kernel=conv_in_kernel, per_token_mod=per_token_mod)
    return pl.pallas_call(
        kern,
        grid=(t // tm,),
        in_specs=specs,
        out_specs=tok(d),
        out_shape=jax.ShapeDtypeStruct((t, d), F32),
        scratch_shapes=scratch,
        compiler_params=_params(1),
        name="post_prompt" if conv_in_kernel else "post_sample",
    )(*args)


def _pack_w_in(w_in, d, dc, dqkv):
    o1 = 2 * dc + dqkv + d
    ba = w_in[:, o1:o1 + 2 * N_HEADS]
    gates = w_in[:, o1 + 2 * N_HEADS:]
    pad = jnp.zeros((w_in.shape[0], LANES - 2 * N_HEADS), w_in.dtype)
    return jnp.concatenate([w_in[:, :o1], gates, ba, pad], axis=1).astype(BF16)


def _lane_pad(vec, offset):
    out = jnp.zeros((1, LANES), F32)
    return out.at[0, offset:offset + vec.shape[0]].set(vec.astype(F32))


def kernel(x_prompt, x_sample, c_prompt, c_sample, state_conf_conv, state_qkv_conv, state_delta, w_ada, b_ada, norm1_w, w_in, conf_dw_w, conf_dw_b, conf_ln_w, conf_ln_b, w_conf_out, gdn_conv_w, a_log, dt_bias, gdn_norm_w, w_gdn_out, w_o, norm2_w, w_ff1, w_ff2, final_norm_w):
    nb, seq_len, d = x_prompt.shape
    ns = x_sample.shape[0]
    depth = w_ada.shape[0]
    dc = conf_dw_w.shape[2]
    dqkv = gdn_conv_w.shape[2]
    n_hist = conf_dw_w.shape[1] - 1
    n_taps = gdn_conv_w.shape[1]
    assert x_sample.shape[1] == 1

    xp = x_prompt.reshape(nb * seq_len, d)
    xs = x_sample.reshape(ns, d)
    c_all = jnp.concatenate([c_prompt, c_sample], axis=0)
    row2 = lambda v: v.reshape(1, -1).astype(F32)

    conf_p, qkv_p, delta_p, conf_s, qkv_s, delta_s = [], [], [], [], [], []
    for l in range(depth):
        mod = _mod_call(c_all, w_ada[l].astype(BF16), row2(b_ada[l]))
        mod_p = mod[:nb].reshape(nb, 1, N_MOD * d)
        mod_s = mod[nb:]
        w_pack = _pack_w_in(w_in[l], d, dc, dqkv)
        n1w, n2w, fnw = row2(norm1_w[l]), row2(norm2_w[l]), row2(final_norm_w)
        alog_p = _lane_pad(a_log[l], N_HEADS)
        dtb_p = _lane_pad(dt_bias[l], N_HEADS)
        cw = gdn_conv_w[l].astype(F32)
        dww, dwb = conf_dw_w[l].astype(F32), row2(conf_dw_b[l])
        lnw, lnb = row2(conf_ln_w[l]), row2(conf_ln_b[l])
        gnw = row2(gdn_norm_w[l])
        wca, wgo, wo = w_conf_out[l].astype(BF16), w_gdn_out[l].astype(BF16), w_o[l].astype(BF16)
        w1, w2 = w_ff1[l].astype(BF16), w_ff2[l].astype(BF16)
        assert depth == 1

        glu, qn, kn, v, z, ga, gb, bg, tail = _pre_prompt_call(xp, mod_p, n1w, w_pack, cw, alog_p, dtb_p, seq_len)
        o, s_fin = _gdn_prompt_call(qn, kn, v, bg, bg.T, nb, seq_len)
        xp = _post_call(xp, mod_p, (glu, dww, dwb), lnw, lnb, o, z, gnw, ga, gb, wca, wgo, wo, n2w, w1, w2, fnw,
                        seq_len=seq_len, per_token_mod=False)
        conf_p.append(glu.reshape(nb, seq_len, dc)[:, seq_len - n_hist:])
        qkv_p.append(tail.reshape(nb, SUBLANES, dqkv)[:, SUBLANES - (n_taps - 1):])
        delta_p.append(s_fin)

        sconf = state_conf_conv[l].reshape(ns, n_hist * dc)
        sqkv = state_qkv_conv[l].reshape(ns, (n_taps - 1) * dqkv)
        apre, qn, kn, v, z, ga, gb, bg, nconf, nqkv = _pre_sample_call(
            xs, mod_s, n1w, w_pack, cw, alog_p, dtb_p, sconf, sqkv, dww, dwb)
        o, s_new = _gdn_sample_call(qn, kn, v, bg, state_delta[l].astype(F32))
        xs = _post_call(xs, mod_s, (apre,), lnw, lnb, o, z, gnw, ga, gb, wca, wgo, wo, n2w, w1, w2, fnw,
                        seq_len=1, per_token_mod=True)
        conf_s.append(nconf.reshape(ns, n_hist, dc))
        qkv_s.append(nqkv.reshape(ns, n_taps - 1, dqkv))
        delta_s.append(s_new)

    return (xp.reshape(nb, seq_len, d), xs.reshape(ns, 1, d),
            jnp.stack(conf_p), jnp.stack(qkv_p), jnp.stack(delta_p),
            jnp.stack(conf_s), jnp.stack(qkv_s), jnp.stack(delta_s))
```

```python
import functools

import jax
import jax.numpy as jnp
from jax import lax
from jax.experimental import pallas as pl
from jax.experimental.pallas import tpu as pltpu

F32 = jnp.float32
BF16 = jnp.bfloat16

EPS = 1e-6
N_MOD = 6
N_HEADS = 8
HEAD_DIM = 128
CHUNK = 64
LANES = 128
SUBLANES = 8
VMEM_LIMIT_BYTES = 56 * 1024 * 1024

PRE_TILE = 256
POST_TILE = 256
GDN_WINDOW = 128
CONV_HALO = 32
SAMPLE_SEQ_BLOCK = 8


def _sigmoid(x):
    return 1.0 / (1.0 + jnp.exp(-x))


def _silu(x):
    return x * _sigmoid(x)


def _softplus(x):
    return jnp.maximum(x, 0.0) + jnp.log(1.0 + jnp.exp(-jnp.abs(x)))


def _dot(a, b):
    return jnp.dot(a, b, preferred_element_type=F32)


def _dot_nt(a, b):
    return lax.dot_general(a, b, (((1,), (1,)), ((), ())), preferred_element_type=F32)


def _resident(shape):
    n = len(shape)
    return pl.BlockSpec(shape, lambda *_: (0,) * n, pipeline_mode=pl.Buffered(1))


def _params(n_grid):
    return pltpu.CompilerParams(dimension_semantics=("arbitrary",) * n_grid,
                                vmem_limit_bytes=VMEM_LIMIT_BYTES)


def _mod_kernel(c_ref, w_ref, b_ref, o_ref):
    c = c_ref[...]
    o_ref[...] = _dot(_silu(c).astype(BF16), w_ref[...]) + b_ref[...]


def _mod_call(c_all, w_ada, b_ada):
    n, d = c_all.shape
    e = w_ada.shape[1]
    tn = e // 4
    return pl.pallas_call(
        _mod_kernel,
        grid=(e // tn,),
        in_specs=[pl.BlockSpec((n, d), lambda j: (0, 0)),
                  pl.BlockSpec((d, tn), lambda j: (0, j)),
                  pl.BlockSpec((1, tn), lambda j: (0, j))],
        out_specs=pl.BlockSpec((n, tn), lambda j: (0, j)),
        out_shape=jax.ShapeDtypeStruct((n, e), F32),
        compiler_params=_params(1),
        name="mod",
    )(c_all, w_ada, b_ada)


def _pre_front(x, shift1, scale1, n1w):
    ms = jnp.mean(x * x, axis=-1, keepdims=True)
    h = x * lax.rsqrt(ms + EPS) * n1w
    return (h * (1.0 + scale1) + shift1).astype(BF16)


def _qkv_finish(y, g, q_ref, k_ref, v_ref):
    s = _silu(y)
    d = q_ref.shape[1]
    n_qk = d // LANES
    if g < 2 * n_qk:
        s = s * lax.rsqrt(jnp.sum(s * s, axis=-1, keepdims=True) + EPS)
    if g < n_qk:
        q_ref[:, g * LANES:(g + 1) * LANES] = s * (HEAD_DIM ** -0.5)
    elif g < 2 * n_qk:
        k_ref[:, (g - n_qk) * LANES:(g - n_qk + 1) * LANES] = s
    else:
        v_ref[:, (g - 2 * n_qk) * LANES:(g - 2 * n_qk + 1) * LANES] = s


def _beta_decay(ba, alog, dtb):
    lane = lax.broadcasted_iota(jnp.int32, ba.shape, 1)
    beta = _sigmoid(ba)
    g = -jnp.exp(alog) * _softplus(ba + dtb)
    return jnp.where(lane < N_HEADS, beta, g)


def _pre_prompt_kernel(x_ref, mod_ref, n1w_ref, w_ref, cw_ref, alog_ref, dtb_ref, cum_ref,
                       glu_ref, q_ref, k_ref, v_ref, z_ref, ga_ref, gb_ref, bg_ref, tail_ref,
                       qkv_scr, *, tm, tiles_per_seq, d, dc, dqkv):
    i = pl.program_id(0)
    hb = _pre_front(x_ref[...], mod_ref[0, :, 0:d], mod_ref[0, :, d:2 * d], n1w_ref[...])

    u = _dot(hb, w_ref[:, 0:2 * dc])
    glu_ref[...] = u[:, :dc] * _sigmoid(u[:, dc:])

    @pl.when(i % tiles_per_seq == 0)
    def _():
        qkv_scr[0:SUBLANES, :] = jnp.zeros((SUBLANES, dqkv), F32)

    o_qkv = 2 * dc
    o_z = o_qkv + dqkv
    step = 512
    n_chunks = dqkv // step
    n_taps = cw_ref.shape[0]

    def project(c):
        cols = slice(c * step, (c + 1) * step)
        qkv_scr[SUBLANES:SUBLANES + tm, cols] = _dot(hb, w_ref[:, o_qkv + c * step:o_qkv + (c + 1) * step])
        tail_ref[:, cols] = qkv_scr[tm:tm + SUBLANES, cols]

    def conv(c):
        for g in range(c * step // LANES, (c + 1) * step // LANES):
            cols = slice(g * LANES, (g + 1) * LANES)
            y = None
            for j in range(n_taps):
                r0 = SUBLANES - (n_taps - 1) + j
                term = cw_ref[j:j + 1, cols] * qkv_scr[r0:r0 + tm, cols]
                y = term if y is None else y + term
            _qkv_finish(y, g, q_ref, k_ref, v_ref)
        cols = slice(c * step, (c + 1) * step)
        qkv_scr[0:SUBLANES, cols] = qkv_scr[tm:tm + SUBLANES, cols]

    pieces = [(ref, o_z + k * d + half * step, half * step, act)
              for k, (ref, act) in enumerate([(z_ref, False), (ga_ref, True), (gb_ref, True)])
              for half in range(d // step)]
    assert len(pieces) == n_chunks

    def side(c):
        ref, o_w, o_out, act = pieces[c]
        r = _dot(hb, w_ref[:, o_w:o_w + step])
        ref[:, o_out:o_out + step] = (_sigmoid(r) if act else r).astype(BF16)

    project(0)
    for c in range(n_chunks):
        if c + 1 < n_chunks:
            project(c + 1)
        side(c)
        conv(c)

    ba = _dot(hb, w_ref[:, o_z + 3 * d:o_z + 3 * d + LANES])
    bg = _beta_decay(ba, alog_ref[...], dtb_ref[...])
    cum = jnp.dot(cum_ref[...], bg, preferred_element_type=F32, precision=lax.Precision.HIGHEST)
    lane = lax.broadcasted_iota(jnp.int32, bg.shape, 1)
    bg_ref[...] = jnp.where(lane < N_HEADS, bg, cum)


def _pre_prompt_call(x, mod_p, n1w, w_pack, cw, alog_p, dtb_p, seq_len):
    t, d = x.shape
    tm = PRE_TILE
    assert seq_len % tm == 0 and tm % CHUNK == 0
    dqkv = cw.shape[1]
    dc = (w_pack.shape[1] - dqkv - 3 * d - LANES) // 2
    nb = t // seq_len
    tiles_per_seq = seq_len // tm
    row = lax.broadcasted_iota(jnp.int32, (tm, tm), 0)
    col = lax.broadcasted_iota(jnp.int32, (tm, tm), 1)
    cum_mat = ((row // CHUNK == col // CHUNK) & (col <= row)).astype(F32)

    tok = lambda w: pl.BlockSpec((tm, w), lambda i: (i, 0))
    kern = functools.partial(_pre_prompt_kernel, tm=tm, tiles_per_seq=tiles_per_seq, d=d, dc=dc, dqkv=dqkv)
    return pl.pallas_call(
        kern,
        grid=(t // tm,),
        in_specs=[tok(d),
                  pl.BlockSpec((1, 1, mod_p.shape[2]), lambda i: (i // tiles_per_seq, 0, 0)),
                  _resident(n1w.shape), _resident(w_pack.shape), _resident(cw.shape),
                  _resident(alog_p.shape), _resident(dtb_p.shape), _resident(cum_mat.shape)],
        out_specs=[tok(dc), tok(d), tok(d), tok(d), tok(d), tok(d), tok(d), tok(LANES),
                   pl.BlockSpec((SUBLANES, dqkv), lambda i: (i // tiles_per_seq, 0))],
        out_shape=[jax.ShapeDtypeStruct((t, dc), F32),
                   jax.ShapeDtypeStruct((t, d), F32), jax.ShapeDtypeStruct((t, d), F32),
                   jax.ShapeDtypeStruct((t, d), F32), jax.ShapeDtypeStruct((t, d), BF16),
                   jax.ShapeDtypeStruct((t, d), BF16), jax.ShapeDtypeStruct((t, d), BF16),
                   jax.ShapeDtypeStruct((t, LANES), F32),
                   jax.ShapeDtypeStruct((nb * SUBLANES, dqkv), F32)],
        scratch_shapes=[pltpu.VMEM((tm + SUBLANES, dqkv), F32)],
        compiler_params=_params(1),
        name="pre_prompt",
    )(x, mod_p, n1w, w_pack, cw, alog_p, dtb_p, cum_mat)


def _pre_sample_kernel(x_ref, mod_ref, n1w_ref, w_ref, cw_ref, alog_ref, dtb_ref,
                       sconf_ref, sqkv_ref, dww_ref, dwb_ref,
                       apre_ref, q_ref, k_ref, v_ref, z_ref, ga_ref, gb_ref, bg_ref,
                       nconf_ref, nqkv_ref, *, d, dc, dqkv):
    hb = _pre_front(x_ref[...], mod_ref[:, 0:d], mod_ref[:, d:2 * d], n1w_ref[...])

    u = _dot(hb, w_ref[:, 0:2 * dc])
    glu = u[:, :dc] * _sigmoid(u[:, dc:])
    n_hist = sconf_ref.shape[1] // dc
    acc = dww_ref[n_hist:n_hist + 1, :] * glu + dwb_ref[...]
    for j in range(n_hist):
        acc = acc + dww_ref[j:j + 1, :] * sconf_ref[:, j * dc:(j + 1) * dc]
    apre_ref[...] = acc
    nconf_ref[:, 0:(n_hist - 1) * dc] = sconf_ref[:, dc:n_hist * dc]
    nconf_ref[:, (n_hist - 1) * dc:n_hist * dc] = glu

    o_qkv = 2 * dc
    n_taps = cw_ref.shape[0]
    nqkv_ref[:, 0:(n_taps - 2) * dqkv] = sqkv_ref[:, dqkv:(n_taps - 1) * dqkv]
    for g in range(dqkv // LANES):
        cols = slice(g * LANES, (g + 1) * LANES)
        raw = _dot(hb, w_ref[:, o_qkv + g * LANES:o_qkv + (g + 1) * LANES])
        nqkv_ref[:, (n_taps - 2) * dqkv + g * LANES:(n_taps - 2) * dqkv + (g + 1) * LANES] = raw
        y = cw_ref[n_taps - 1:n_taps, cols] * raw
        for j in range(n_taps - 1):
            y = y + cw_ref[j:j + 1, cols] * sqkv_ref[:, j * dqkv + g * LANES:j * dqkv + (g + 1) * LANES]
        _qkv_finish(y, g, q_ref, k_ref, v_ref)

    o_z = o_qkv + dqkv
    z_ref[...] = _dot(hb, w_ref[:, o_z:o_z + d]).astype(BF16)
    ga_ref[...] = _sigmoid(_dot(hb, w_ref[:, o_z + d:o_z + 2 * d])).astype(BF16)
    gb_ref[...] = _sigmoid(_dot(hb, w_ref[:, o_z + 2 * d:o_z + 3 * d])).astype(BF16)
    ba = _dot(hb, w_ref[:, o_z + 3 * d:o_z + 3 * d + LANES])
    bg_ref[...] = _beta_decay(ba, alog_ref[...], dtb_ref[...])


def _pre_sample_call(x, mod_s, n1w, w_pack, cw, alog_p, dtb_p, sconf, sqkv, dww, dwb):
    n, d = x.shape
    dqkv = cw.shape[1]
    dc = dwb.shape[1]
    tm = 32
    assert n % tm == 0
    tok = lambda w: pl.BlockSpec((tm, w), lambda i: (i, 0))
    kern = functools.partial(_pre_sample_kernel, d=d, dc=dc, dqkv=dqkv)
    return pl.pallas_call(
        kern,
        grid=(n // tm,),
        in_specs=[tok(d), tok(mod_s.shape[1]),
                  _resident(n1w.shape), _resident(w_pack.shape), _resident(cw.shape),
                  _resident(alog_p.shape), _resident(dtb_p.shape),
                  tok(sconf.shape[1]), tok(sqkv.shape[1]),
                  _resident(dww.shape), _resident(dwb.shape)],
        out_specs=[tok(dc), tok(d), tok(d), tok(d), tok(d), tok(d), tok(d), tok(LANES),
                   tok(sconf.shape[1]), tok(sqkv.shape[1])],
        out_shape=[jax.ShapeDtypeStruct((n, dc), F32),
                   jax.ShapeDtypeStruct((n, d), F32), jax.ShapeDtypeStruct((n, d), F32),
                   jax.ShapeDtypeStruct((n, d), F32), jax.ShapeDtypeStruct((n, d), BF16),
                   jax.ShapeDtypeStruct((n, d), BF16), jax.ShapeDtypeStruct((n, d), BF16),
                   jax.ShapeDtypeStruct((n, LANES), F32),
                   jax.ShapeDtypeStruct(sconf.shape, F32), jax.ShapeDtypeStruct(sqkv.shape, F32)],
        compiler_params=_params(1),
        name="pre_sample",
    )(x, mod_s, n1w, w_pack, cw, alog_p, dtb_p, sconf, sqkv, dww, dwb)


def _half_block_mask(row, col, size):
    shift = size.bit_length() - 1
    half = size // 2
    return ((lax.shift_right_logical(row, shift) == lax.shift_right_logical(col, shift))
            & ((row & half) != 0) & ((col & half) == 0))


def _block_diag2(a, b):
    z = jnp.zeros(a.shape, a.dtype)
    return jnp.concatenate([jnp.concatenate([a, z], axis=1), jnp.concatenate([z, b], axis=1)], axis=0)


def _pair_dot(a, b):
    n = b[0].shape[1]
    out = _dot(jnp.concatenate(a, axis=1), _block_diag2(b[0], b[1]))
    return [out[:, :n], out[:, n:]]


def _gdn_prompt_kernel(q_ref, k_ref, v_ref, bg_ref, bgt_ref, o_ref, sfin_ref, s_scr, *, w, n_win):
    wi = pl.program_id(1)

    @pl.when(wi == 0)
    def _():
        s_scr[...] = jnp.zeros(s_scr.shape, F32)

    nc = w // CHUNK
    heads = range(N_HEADS)
    pairs = range(0, N_HEADS, 2)
    row = lax.broadcasted_iota(jnp.int32, (w, w), 0)
    col = lax.broadcasted_iota(jnp.int32, (w, w), 1)
    shift = CHUNK.bit_length() - 1
    same = lax.shift_right_logical(row, shift) == lax.shift_right_logical(col, shift)
    m_incl = same & (col <= row)
    m_strict = same & (col < row)
    eye = (row == col).astype(F32)

    hs = [slice(h * HEAD_DIM, (h + 1) * HEAD_DIM) for h in heads]
    qh = [q_ref[:, hs[h]] for h in heads]
    kh = [k_ref[:, hs[h]] for h in heads]
    vh = [v_ref[:, hs[h]] for h in heads]
    bcol = [jnp.broadcast_to(bg_ref[:, h:h + 1], (w, HEAD_DIM)) for h in heads]
    gcb = [jnp.broadcast_to(bg_ref[:, N_HEADS + h:N_HEADS + h + 1], (w, HEAD_DIM)) for h in heads]
    grow = [bgt_ref[N_HEADS + h:N_HEADS + h + 1, :] for h in heads]
    decay = [jnp.exp(jnp.where(m_incl, gcb[h] - grow[h], -1e30)) for h in heads]

    kb = [kh[h].astype(BF16) for h in heads]
    qb = [qh[h].astype(BF16) for h in heads]
    amat, att = [None] * N_HEADS, [None] * N_HEADS
    for p in pairs:
        lhs = jnp.concatenate([jnp.concatenate([kb[p], kb[p + 1]], axis=1),
                               jnp.concatenate([qb[p], qb[p + 1]], axis=1)], axis=0)
        r = _dot_nt(lhs, _block_diag2(kb[p], kb[p + 1]))
        for j in range(2):
            kk = r[:w, j * w:(j + 1) * w]
            amat[p + j] = jnp.where(m_strict, kk * bcol[p + j] * decay[p + j], 0.0)
            att[p + j] = (r[w:, j * w:(j + 1) * w] * decay[p + j]).astype(BF16)

    x = [eye - jnp.where(_half_block_mask(row, col, 2), amat[h], 0.0) for h in heads]
    size = 4
    while size <= CHUNK:
        mask = _half_block_mask(row, col, size)
        for p in pairs:
            blk = [jnp.where(mask, amat[p + j], 0.0).astype(BF16) for j in range(2)]
            xb = [x[p + j].astype(BF16) for j in range(2)]
            y = _pair_dot(blk, xb)
            upd = _pair_dot(xb, [y[j].astype(BF16) for j in range(2)])
            for j in range(2):
                x[p + j] = x[p + j] - upd[j]
        size *= 2

    eg = [jnp.exp(gcb[h]) for h in heads]
    value, kcd, qe, glast, ktt = [], [], [], [], []
    for h in heads:
        rhs = jnp.concatenate([vh[h] * bcol[h], kh[h] * (bcol[h] * eg[h])], axis=1).astype(BF16)
        vk = _dot(x[h].astype(BF16), rhs)
        value.append(vk[:, :HEAD_DIM])
        kcd.append(vk[:, HEAD_DIM:])
        qe.append(qh[h] * eg[h])
        gl = jnp.concatenate(
            [jnp.broadcast_to(gcb[h][(c + 1) * CHUNK - 1:(c + 1) * CHUNK, :], (CHUNK, HEAD_DIM)) for c in range(nc)],
            axis=0)
        glast.append(gl)
        ktt.append((kh[h] * jnp.exp(gl - gcb[h])).T.astype(BF16))

    s = [s_scr[h] for h in heads]
    for c in range(nc):
        cs = slice(c * CHUNK, (c + 1) * CHUNK)
        for p in pairs:
            lhs = [jnp.concatenate([kcd[p + j][cs], qe[p + j][cs]], axis=0).astype(BF16) for j in range(2)]
            ps = _pair_dot(lhs, [s[p + j].astype(BF16) for j in range(2)])
            vpad = []
            for j in range(2):
                parts = [jnp.zeros((CHUNK, HEAD_DIM), F32)] * nc
                parts[c] = value[p + j][cs] - ps[j][:CHUNK]
                vpad.append(jnp.concatenate(parts, axis=0).astype(BF16))
            intra = _pair_dot([att[p + j][cs] for j in range(2)], vpad)
            outer = _pair_dot([ktt[p + j] for j in range(2)], vpad)
            for j in range(2):
                o_ref[cs, hs[p + j]] = ps[j][CHUNK:] + intra[j]
                s[p + j] = s[p + j] * jnp.exp(glast[p + j][c * CHUNK:c * CHUNK + 1, :]) + outer[j]
    for h in heads:
        s_scr[h] = s[h]

    @pl.when(wi == n_win - 1)
    def _():
        sfin_ref[0] = s_scr[...]


def _gdn_prompt_call(qn, kn, v, bg, bgt, nb, seq_len):
    t, d = qn.shape
    w = GDN_WINDOW
    assert seq_len % w == 0 and w % CHUNK == 0 and w == HEAD_DIM
    n_win = seq_len // w
    tok = pl.BlockSpec((w, d), lambda b, i: (b * n_win + i, 0))
    kern = functools.partial(_gdn_prompt_kernel, w=w, n_win=n_win)
    return pl.pallas_call(
        kern,
        grid=(nb, n_win),
        in_specs=[tok, tok, tok,
                  pl.BlockSpec((w, LANES), lambda b, i: (b * n_win + i, 0)),
                  pl.BlockSpec((LANES, w), lambda b, i: (0, b * n_win + i))],
        out_specs=[tok, pl.BlockSpec((1, N_HEADS, HEAD_DIM, HEAD_DIM), lambda b, i: (b, 0, 0, 0))],
        out_shape=[jax.ShapeDtypeStruct((t, d), F32),
                   jax.ShapeDtypeStruct((nb, N_HEADS, HEAD_DIM, HEAD_DIM), F32)],
        scratch_shapes=[pltpu.VMEM((N_HEADS, HEAD_DIM, HEAD_DIM), F32)],
        compiler_params=_params(2),
        name="gdn_prompt",
    )(qn, kn, v, bg, bgt)


def _gdn_sample_kernel(q_ref, k_ref, v_ref, bg_ref, s_ref, o_ref, snew_ref, *, bs):
    row = lax.broadcasted_iota(jnp.int32, (HEAD_DIM, HEAD_DIM), 0)
    col = lax.broadcasted_iota(jnp.int32, (HEAD_DIM, HEAD_DIM), 1)
    eye = row == col
    for h in range(N_HEADS):
        hs = slice(h * HEAD_DIM, (h + 1) * HEAD_DIM)
        qh = q_ref[:, hs]
        kh = k_ref[:, hs]
        vh = v_ref[:, hs]
        beta = jnp.broadcast_to(bg_ref[:, h:h + 1], (bs, HEAD_DIM))
        eg = jnp.exp(jnp.broadcast_to(bg_ref[:, N_HEADS + h:N_HEADS + h + 1], (bs, HEAD_DIM)))
        att = jnp.sum(qh * kh, axis=-1, keepdims=True)
        for i in range(bs):
            s = s_ref[i, h]
            kcol = jnp.sum(jnp.where(eye, jnp.broadcast_to(kh[i:i + 1, :], (HEAD_DIM, HEAD_DIM)), 0.0),
                           axis=-1, keepdims=True)
            qcol = jnp.sum(jnp.where(eye, jnp.broadcast_to(qh[i:i + 1, :], (HEAD_DIM, HEAD_DIM)), 0.0),
                           axis=-1, keepdims=True)
            ks = jnp.sum(kcol * s, axis=0, keepdims=True)
            qs = jnp.sum(qcol * s, axis=0, keepdims=True)
            egi = eg[i:i + 1, :]
            v_new = beta[i:i + 1, :] * (vh[i:i + 1, :] - egi * ks)
            o_ref[i:i + 1, hs] = egi * qs + att[i:i + 1, :] * v_new
            snew_ref[i, h] = s * egi + kcol * v_new


def _gdn_sample_call(qn, kn, v, bg, s0):
    n, d = qn.shape
    bs = SAMPLE_SEQ_BLOCK
    assert n % bs == 0
    tok = lambda w: pl.BlockSpec((bs, w), lambda i: (i, 0))
    st = pl.BlockSpec((bs, N_HEADS, HEAD_DIM, HEAD_DIM), lambda i: (i, 0, 0, 0))
    return pl.pallas_call(
        functools.partial(_gdn_sample_kernel, bs=bs),
        grid=(n // bs,),
        in_specs=[tok(d), tok(d), tok(d), tok(LANES), st],
        out_specs=[tok(d), st],
        out_shape=[jax.ShapeDtypeStruct((n, d), F32), jax.ShapeDtypeStruct(s0.shape, F32)],
        compiler_params=_params(1),
        name="gdn_sample",
    )(qn, kn, v, bg, s0)


def _rms(x, w):
    return x * lax.rsqrt(jnp.mean(x * x, axis=-1, keepdims=True) + EPS) * w


def _ln_swish(a, lnw_ref, lnb_ref):
    mu = jnp.mean(a, axis=-1, keepdims=True)
    ac = a - mu
    var = jnp.mean(ac * ac, axis=-1, keepdims=True)
    return _silu(ac * lax.rsqrt(var + EPS) * lnw_ref[...] + lnb_ref[...]).astype(BF16)


def _conv_ln_swish_jobs(ext_scr, sh_scr, cv_scr, a_scr, dww_ref, dwb_ref, lnw_ref, lnb_ref, *, tm, dc):
    n_taps = dww_ref.shape[0]
    rows = tm + CONV_HALO - SUBLANES

    def shifts():
        for r in range(1, SUBLANES):
            sh_scr[r - 1] = ext_scr[r:r + rows, :]

    def taps(g):
        cols = slice(g * LANES, (g + 1) * LANES)
        acc = None
        for j in range(n_taps):
            off = CONV_HALO - (n_taps - 1) + j
            r, base = off % SUBLANES, off - off % SUBLANES
            src = ext_scr[base:base + tm, cols] if r == 0 else sh_scr[r - 1, base:base + tm, cols]
            term = dww_ref[j:j + 1, cols] * src
            acc = term if acc is None else acc + term
        cv_scr[:, cols] = acc + dwb_ref[:, cols]

    def finish():
        a_scr[...] = _ln_swish(cv_scr[...], lnw_ref, lnb_ref)

    return [shifts] + [functools.partial(taps, g) for g in range(dc // LANES)] + [finish]


def _post_tail(a, x_ref, mod, o_ref, z_ref, gnw_ref, ga_ref, gb_ref, wca_ref, wgo_ref, wo_ref, n2w_ref,
               w1_ref, w2_ref, fnw_ref, y_ref, *, dff, side_jobs=()):
    side = list(side_jobs)

    def run_side():
        if side:
            side.pop(0)()

    y_a = _dot(a, wca_ref[...])
    run_side()

    gated = []
    for h in range(N_HEADS):
        hs = slice(h * HEAD_DIM, (h + 1) * HEAD_DIM)
        gated.append(_rms(o_ref[:, hs], gnw_ref[...]) * _silu(z_ref[:, hs].astype(F32)))
    y_b = _dot(jnp.concatenate(gated, axis=1).astype(BF16), wgo_ref[...])
    run_side()

    merged = ga_ref[...].astype(F32) * y_a + gb_ref[...].astype(F32) * y_b
    x1 = x_ref[...] + mod(2) * _dot(merged.astype(BF16), wo_ref[...])
    run_side()

    h2 = (_rms(x1, n2w_ref[...]) * (1.0 + mod(4)) + mod(3)).astype(BF16)
    step = 1024
    ff = None
    for c in range(dff // step):
        f = jnp.maximum(_dot(h2, w1_ref[:, c * step:(c + 1) * step]), 0.0)
        part = _dot((f * f).astype(BF16), w2_ref[c * step:(c + 1) * step, :])
        ff = part if ff is None else ff + part
        run_side()
    while side:
        run_side()
    x2 = x1 + mod(5) * ff
    y_ref[...] = _rms(x2, fnw_ref[...])


def _post_prompt_kernel(x_ref, mod_ref, glu_ref, glun_ref, dww_ref, dwb_ref, lnw_ref, lnb_ref,
                        o_ref, z_ref, gnw_ref, ga_ref, gb_ref, wca_ref, wgo_ref, wo_ref, n2w_ref,
                        w1_ref, w2_ref, fnw_ref, y_ref, ext_scr, sh_scr, cv_scr, a_scr,
                        *, tm, tiles_per_seq, d, dc, dff):
    i = pl.program_id(0)
    conv_jobs = functools.partial(_conv_ln_swish_jobs, ext_scr, sh_scr, cv_scr, a_scr, dww_ref, dwb_ref,
                                  lnw_ref, lnb_ref, tm=tm, dc=dc)

    @pl.when(i == 0)
    def _():
        ext_scr[0:CONV_HALO, :] = jnp.zeros((CONV_HALO, dc), F32)
        ext_scr[CONV_HALO:CONV_HALO + tm, :] = glu_ref[...]
        for job in conv_jobs():
            job()

    a = a_scr[...]
    ext_scr[0:CONV_HALO, :] = jnp.where((i + 1) % tiles_per_seq == 0, 0.0, glu_ref[tm - CONV_HALO:tm, :])
    ext_scr[CONV_HALO:CONV_HALO + tm, :] = glun_ref[...]

    mod = lambda k: mod_ref[0, :, k * d:(k + 1) * d]
    _post_tail(a, x_ref, mod, o_ref, z_ref, gnw_ref, ga_ref, gb_ref, wca_ref, wgo_ref, wo_ref, n2w_ref,
               w1_ref, w2_ref, fnw_ref, y_ref, dff=dff, side_jobs=conv_jobs())


def _post_sample_kernel(x_ref, mod_ref, apre_ref, lnw_ref, lnb_ref,
                        o_ref, z_ref, gnw_ref, ga_ref, gb_ref, wca_ref, wgo_ref, wo_ref, n2w_ref,
                        w1_ref, w2_ref, fnw_ref, y_ref, *, d, dff):
    a = _ln_swish(apre_ref[...], lnw_ref, lnb_ref)
    mod = lambda k: mod_ref[:, k * d:(k + 1) * d]
    _post_tail(a, x_ref, mod, o_ref, z_ref, gnw_ref, ga_ref, gb_ref, wca_ref, wgo_ref, wo_ref, n2w_ref,
               w1_ref, w2_ref, fnw_ref, y_ref, dff=dff)


def _post_call(x, mod, conv_in, lnw, lnb, o, z, gnw, ga, gb, wca, wgo, wo, n2w, w1, w2, fnw, *, seq_len):
    t, d = x.shape
    dc = lnw.shape[1]
    dff = w1.shape[1]
    prompt = len(conv_in) == 3
    tm = min(POST_TILE, t)
    assert t % tm == 0
    n_tiles = t // tm
    tok = lambda w: pl.BlockSpec((tm, w), lambda i: (i, 0))
    common_args = [lnw, lnb, o, z, gnw, ga, gb, wca, wgo, wo, n2w, w1, w2, fnw]
    common_specs = [_resident(lnw.shape), _resident(lnb.shape), tok(d), tok(d), _resident(gnw.shape), tok(d), tok(d),
                    _resident(wca.shape), _resident(wgo.shape), _resident(wo.shape), _resident(n2w.shape),
                    _resident(w1.shape), _resident(w2.shape), _resident(fnw.shape)]
    if prompt:
        glu, dww, dwb = conv_in
        assert seq_len % tm == 0 and tm >= CONV_HALO >= dww.shape[0] - 1
        tiles_per_seq = seq_len // tm
        args = [x, mod, glu, glu, dww, dwb] + common_args
        specs = [tok(d), pl.BlockSpec((1, 1, mod.shape[2]), lambda i: (i // tiles_per_seq, 0, 0)),
                 tok(dc), pl.BlockSpec((tm, dc), lambda i: (jnp.minimum(i + 1, n_tiles - 1), 0)),
                 _resident(dww.shape), _resident(dwb.shape)] + common_specs
        scratch = [pltpu.VMEM((tm + CONV_HALO, dc), F32),
                   pltpu.VMEM((SUBLANES - 1, tm + CONV_HALO - SUBLANES, dc), F32),
                   pltpu.VMEM((tm, dc), F32),
                   pltpu.VMEM((tm, dc), BF16)]
        kern = functools.partial(_post_prompt_kernel, tm=tm, tiles_per_seq=tiles_per_seq, d=d, dc=dc, dff=dff)
        name = "post_prompt"
    else:
        args = [x, mod, conv_in[0]] + common_args
        specs = [tok(d), tok(mod.shape[1]), tok(dc)] + common_specs
        scratch = []
        kern = functools.partial(_post_sample_kernel, d=d, dff=dff)
        name = "post_sample"
    return pl.pallas_call(
        kern,
        grid=(n_tiles,),
        in_specs=specs,
        out_specs=tok(d),
        out_shape=jax.ShapeDtypeStruct((t, d), F32),
        scratch_shapes=scratch,
        compiler_params=_params(1),
        name=name,
    )(*args)


def _pack_w_in(w_in, d, dc, dqkv):
    o1 = 2 * dc + dqkv + d
    ba = w_in[:, o1:o1 + 2 * N_HEADS]
    gates = w_in[:, o1 + 2 * N_HEADS:]
    pad = jnp.zeros((w_in.shape[0], LANES - 2 * N_HEADS), w_in.dtype)
    return jnp.concatenate([w_in[:, :o1], gates, ba, pad], axis=1).astype(BF16)


def _lane_pad(vec, offset):
    out = jnp.zeros((1, LANES), F32)
    return out.at[0, offset:offset + vec.shape[0]].set(vec.astype(F32))


def kernel(x_prompt, x_sample, c_prompt, c_sample, state_conf_conv, state_qkv_conv, state_delta, w_ada, b_ada, norm1_w, w_in, conf_dw_w, conf_dw_b, conf_ln_w, conf_ln_b, w_conf_out, gdn_conv_w, a_log, dt_bias, gdn_norm_w, w_gdn_out, w_o, norm2_w, w_ff1, w_ff2, final_norm_w):
    nb, seq_len, d = x_prompt.shape
    ns = x_sample.shape[0]
    depth = w_ada.shape[0]
    dc = conf_dw_w.shape[2]
    dqkv = gdn_conv_w.shape[2]
    n_hist = conf_dw_w.shape[1] - 1
    n_taps = gdn_conv_w.shape[1]
    assert x_sample.shape[1] == 1

    xp = x_prompt.reshape(nb * seq_len, d)
    xs = x_sample.reshape(ns, d)
    c_all = jnp.concatenate([c_prompt, c_sample], axis=0)
    row2 = lambda v: v.reshape(1, -1).astype(F32)

    conf_p, qkv_p, delta_p, conf_s, qkv_s, delta_s = [], [], [], [], [], []
    for l in range(depth):
        mod = _mod_call(c_all, w_ada[l].astype(BF16), row2(b_ada[l]))
        mod_p = mod[:nb].reshape(nb, 1, N_MOD * d)
        mod_s = mod[nb:]
        w_pack = _pack_w_in(w_in[l], d, dc, dqkv)
        n1w, n2w, fnw = row2(norm1_w[l]), row2(norm2_w[l]), row2(final_norm_w)
        alog_p = _lane_pad(a_log[l], N_HEADS)
        dtb_p = _lane_pad(dt_bias[l], N_HEADS)
        cw = gdn_conv_w[l].astype(F32)
        dww, dwb = conf_dw_w[l].astype(F32), row2(conf_dw_b[l])
        lnw, lnb = row2(conf_ln_w[l]), row2(conf_ln_b[l])
        gnw = row2(gdn_norm_w[l])
        wca, wgo, wo = w_conf_out[l].astype(BF16), w_gdn_out[l].astype(BF16), w_o[l].astype(BF16)
        w1, w2 = w_ff1[l].astype(BF16), w_ff2[l].astype(BF16)
        assert depth == 1

        glu, qn, kn, v, z, ga, gb, bg, tail = _pre_prompt_call(xp, mod_p, n1w, w_pack, cw, alog_p, dtb_p, seq_len)
        o, s_fin = _gdn_prompt_call(qn, kn, v, bg, bg.T, nb, seq_len)
        xp = _post_call(xp, mod_p, (glu, dww, dwb), lnw, lnb, o, z, gnw, ga, gb, wca, wgo, wo, n2w, w1, w2, fnw,
                        seq_len=seq_len)
        conf_p.append(glu.reshape(nb, seq_len, dc)[:, seq_len - n_hist:])
        qkv_p.append(tail.reshape(nb, SUBLANES, dqkv)[:, SUBLANES - (n_taps - 1):])
        delta_p.append(s_fin)

        sconf = state_conf_conv[l].reshape(ns, n_hist * dc)
        sqkv = state_qkv_conv[l].reshape(ns, (n_taps - 1) * dqkv)
        apre, qn, kn, v, z, ga, gb, bg, nconf, nqkv = _pre_sample_call(
            xs, mod_s, n1w, w_pack, cw, alog_p, dtb_p, sconf, sqkv, dww, dwb)
        o, s_new = _gdn_sample_call(qn, kn, v, bg, state_delta[l].astype(F32))
        xs = _post_call(xs, mod_s, (apre,), lnw, lnb, o, z, gnw, ga, gb, wca, wgo, wo, n2w, w1, w2, fnw,
                        seq_len=1)
        conf_s.append(nconf.reshape(ns, n_hist, dc))
        qkv_s.append(nqkv.reshape(ns, n_taps - 1, dqkv))
        delta_s.append(s_new)

    return (xp.reshape(nb, seq_len, d), xs.reshape(ns, 1, d),
            jnp.stack(conf_p), jnp.stack(qkv_p), jnp.stack(delta_p),
            jnp.stack(conf_s), jnp.stack(qkv_s), jnp.stack(delta_s))
```

```python
import functools

import jax
import jax.numpy as jnp
from jax import lax
from jax.experimental import pallas as pl
from jax.experimental.pallas import tpu as pltpu

F32 = jnp.float32
BF16 = jnp.bfloat16

EPS = 1e-6
N_MOD = 6
N_HEADS = 8
HEAD_DIM = 128
CHUNK = 64
LANES = 128
SUBLANES = 8
VMEM_LIMIT_BYTES = 56 * 1024 * 1024

PRE_TILE = 256
POST_TILE = 256
GDN_WINDOW = 128
GDN_WINDOWS_PER_STEP = 4
CONV_HALO = 32
SAMPLE_SEQ_BLOCK = 8


NEG_LOG2_E = -1.4426950408889634


def _sigmoid(x):
    return 1.0 / (1.0 + jnp.exp2(x * NEG_LOG2_E))


def _silu(x):
    return x * _sigmoid(x)


def _softplus(x):
    return jnp.maximum(x, 0.0) + jnp.log(1.0 + jnp.exp(-jnp.abs(x)))


def _dot(a, b):
    return jnp.dot(a, b, preferred_element_type=F32)


def _dot_nt(a, b):
    return lax.dot_general(a, b, (((1,), (1,)), ((), ())), preferred_element_type=F32)


def _resident(shape):
    n = len(shape)
    return pl.BlockSpec(shape, lambda *_: (0,) * n, pipeline_mode=pl.Buffered(1))


def _params(n_grid):
    return pltpu.CompilerParams(dimension_semantics=("arbitrary",) * n_grid,
                                vmem_limit_bytes=VMEM_LIMIT_BYTES)


def _mod_kernel(c_ref, w_ref, b_ref, o_ref):
    c = c_ref[...]
    o_ref[...] = _dot(_silu(c).astype(BF16), w_ref[...]) + b_ref[...]


def _mod_call(c_all, w_ada, b_ada):
    n, d = c_all.shape
    e = w_ada.shape[1]
    tn = e // 4
    return pl.pallas_call(
        _mod_kernel,
        grid=(e // tn,),
        in_specs=[pl.BlockSpec((n, d), lambda j: (0, 0)),
                  pl.BlockSpec((d, tn), lambda j: (0, j)),
                  pl.BlockSpec((1, tn), lambda j: (0, j))],
        out_specs=pl.BlockSpec((n, tn), lambda j: (0, j)),
        out_shape=jax.ShapeDtypeStruct((n, e), F32),
        compiler_params=_params(1),
        name="mod",
    )(c_all, w_ada, b_ada)


def _pre_front(x, shift1, scale1, n1w):
    ms = jnp.mean(x * x, axis=-1, keepdims=True)
    h = x * lax.rsqrt(ms + EPS) * n1w
    return (h * (1.0 + scale1) + shift1).astype(BF16)


def _qkv_finish(y, g, q_ref, k_ref, v_ref):
    s = _silu(y)
    d = q_ref.shape[1]
    n_qk = d // LANES
    if g < 2 * n_qk:
        s = s * lax.rsqrt(jnp.sum(s * s, axis=-1, keepdims=True) + EPS)
    if g < n_qk:
        q_ref[:, g * LANES:(g + 1) * LANES] = s * (HEAD_DIM ** -0.5)
    elif g < 2 * n_qk:
        k_ref[:, (g - n_qk) * LANES:(g - n_qk + 1) * LANES] = s
    else:
        v_ref[:, (g - 2 * n_qk) * LANES:(g - 2 * n_qk + 1) * LANES] = s


def _beta_decay(ba, alog, dtb):
    lane = lax.broadcasted_iota(jnp.int32, ba.shape, 1)
    beta = _sigmoid(ba)
    g = -jnp.exp(alog) * _softplus(ba + dtb)
    return jnp.where(lane < N_HEADS, beta, g)


def _pre_prompt_kernel(x_ref, mod_ref, n1w_ref, w_ref, cw_ref, alog_ref, dtb_ref, cum_ref,
                       glu_ref, q_ref, k_ref, v_ref, z_ref, ga_ref, gb_ref, bg_ref, bgt_ref, tail_ref,
                       qkv_scr, h_scr, *, tm, tiles_per_seq, d, dc, dqkv):
    i = pl.program_id(0)
    h_scr[...] = _pre_front(x_ref[...], mod_ref[0, :, 0:d], mod_ref[0, :, d:2 * d], n1w_ref[...])

    u = _dot(h_scr[...], w_ref[:, 0:2 * dc])
    glu_ref[...] = u[:, :dc] * _sigmoid(u[:, dc:])

    @pl.when(i % tiles_per_seq == 0)
    def _():
        qkv_scr[0:SUBLANES, :] = jnp.zeros((SUBLANES, dqkv), F32)

    o_qkv = 2 * dc
    o_z = o_qkv + dqkv
    step = 512
    n_chunks = dqkv // step
    n_taps = cw_ref.shape[0]

    def project(c):
        cols = slice(c * step, (c + 1) * step)
        qkv_scr[SUBLANES:SUBLANES + tm, cols] = _dot(h_scr[...], w_ref[:, o_qkv + c * step:o_qkv + (c + 1) * step])
        tail_ref[:, cols] = qkv_scr[tm:tm + SUBLANES, cols]

    def conv(c):
        for g in range(c * step // LANES, (c + 1) * step // LANES):
            cols = slice(g * LANES, (g + 1) * LANES)
            y = None
            for j in range(n_taps):
                r0 = SUBLANES - (n_taps - 1) + j
                term = cw_ref[j:j + 1, cols] * qkv_scr[r0:r0 + tm, cols]
                y = term if y is None else y + term
            _qkv_finish(y, g, q_ref, k_ref, v_ref)
        cols = slice(c * step, (c + 1) * step)
        qkv_scr[0:SUBLANES, cols] = qkv_scr[tm:tm + SUBLANES, cols]

    pieces = [(ref, o_z + k * d + half * step, half * step, act)
              for k, (ref, act) in enumerate([(z_ref, False), (ga_ref, True), (gb_ref, True)])
              for half in range(d // step)]
    assert len(pieces) == n_chunks

    def side(c):
        ref, o_w, o_out, act = pieces[c]
        r = _dot(h_scr[...], w_ref[:, o_w:o_w + step])
        ref[:, o_out:o_out + step] = (_sigmoid(r) if act else r).astype(BF16)

    project(0)
    for c in range(n_chunks):
        if c + 1 < n_chunks:
            project(c + 1)
        side(c)
        conv(c)

    ba = _dot(h_scr[...], w_ref[:, o_z + 3 * d:o_z + 3 * d + LANES])
    bg = _beta_decay(ba, alog_ref[...], dtb_ref[...])
    cum = jnp.dot(cum_ref[...], bg, preferred_element_type=F32, precision=lax.Precision.HIGHEST)
    lane = lax.broadcasted_iota(jnp.int32, bg.shape, 1)
    bg = jnp.where(lane < N_HEADS, bg, cum)
    bg_ref[...] = bg
    bgt_ref[...] = bg.T


def _pre_prompt_call(x, mod_p, n1w, w_pack, cw, alog_p, dtb_p, seq_len):
    t, d = x.shape
    tm = PRE_TILE
    assert seq_len % tm == 0 and tm % CHUNK == 0
    dqkv = cw.shape[1]
    dc = (w_pack.shape[1] - dqkv - 3 * d - LANES) // 2
    nb = t // seq_len
    tiles_per_seq = seq_len // tm
    row = lax.broadcasted_iota(jnp.int32, (tm, tm), 0)
    col = lax.broadcasted_iota(jnp.int32, (tm, tm), 1)
    cum_mat = ((row // CHUNK == col // CHUNK) & (col <= row)).astype(F32)

    tok = lambda w: pl.BlockSpec((tm, w), lambda i: (i, 0))
    kern = functools.partial(_pre_prompt_kernel, tm=tm, tiles_per_seq=tiles_per_seq, d=d, dc=dc, dqkv=dqkv)
    return pl.pallas_call(
        kern,
        grid=(t // tm,),
        in_specs=[tok(d),
                  pl.BlockSpec((1, 1, mod_p.shape[2]), lambda i: (i // tiles_per_seq, 0, 0)),
                  _resident(n1w.shape), _resident(w_pack.shape), _resident(cw.shape),
                  _resident(alog_p.shape), _resident(dtb_p.shape), _resident(cum_mat.shape)],
        out_specs=[tok(dc), tok(d), tok(d), tok(d), tok(d), tok(d), tok(d), tok(LANES),
                   pl.BlockSpec((LANES, tm), lambda i: (0, i)),
                   pl.BlockSpec((SUBLANES, dqkv), lambda i: (i // tiles_per_seq, 0))],
        out_shape=[jax.ShapeDtypeStruct((t, dc), F32),
                   jax.ShapeDtypeStruct((t, d), F32), jax.ShapeDtypeStruct((t, d), F32),
                   jax.ShapeDtypeStruct((t, d), F32), jax.ShapeDtypeStruct((t, d), BF16),
                   jax.ShapeDtypeStruct((t, d), BF16), jax.ShapeDtypeStruct((t, d), BF16),
                   jax.ShapeDtypeStruct((t, LANES), F32),
                   jax.ShapeDtypeStruct((LANES, t), F32),
                   jax.ShapeDtypeStruct((nb * SUBLANES, dqkv), F32)],
        scratch_shapes=[pltpu.VMEM((tm + SUBLANES, dqkv), F32), pltpu.VMEM((tm, d), BF16)],
        compiler_params=_params(1),
        name="pre_prompt",
    )(x, mod_p, n1w, w_pack, cw, alog_p, dtb_p, cum_mat)


def _pre_sample_kernel(x_ref, mod_ref, n1w_ref, w_ref, cw_ref, alog_ref, dtb_ref,
                       sconf_ref, sqkv_ref, dww_ref, dwb_ref,
                       apre_ref, q_ref, k_ref, v_ref, z_ref, ga_ref, gb_ref, bg_ref,
                       nconf_ref, nqkv_ref, *, d, dc, dqkv):
    hb = _pre_front(x_ref[...], mod_ref[:, 0:d], mod_ref[:, d:2 * d], n1w_ref[...])

    u = _dot(hb, w_ref[:, 0:2 * dc])
    glu = u[:, :dc] * _sigmoid(u[:, dc:])
    n_hist = sconf_ref.shape[1] // dc
    acc = dww_ref[n_hist:n_hist + 1, :] * glu + dwb_ref[...]
    for j in range(n_hist):
        acc = acc + dww_ref[j:j + 1, :] * sconf_ref[:, j * dc:(j + 1) * dc]
    apre_ref[...] = acc
    nconf_ref[:, 0:(n_hist - 1) * dc] = sconf_ref[:, dc:n_hist * dc]
    nconf_ref[:, (n_hist - 1) * dc:n_hist * dc] = glu

    o_qkv = 2 * dc
    n_taps = cw_ref.shape[0]
    nqkv_ref[:, 0:(n_taps - 2) * dqkv] = sqkv_ref[:, dqkv:(n_taps - 1) * dqkv]
    for g in range(dqkv // LANES):
        cols = slice(g * LANES, (g + 1) * LANES)
        raw = _dot(hb, w_ref[:, o_qkv + g * LANES:o_qkv + (g + 1) * LANES])
        nqkv_ref[:, (n_taps - 2) * dqkv + g * LANES:(n_taps - 2) * dqkv + (g + 1) * LANES] = raw
        y = cw_ref[n_taps - 1:n_taps, cols] * raw
        for j in range(n_taps - 1):
            y = y + cw_ref[j:j + 1, cols] * sqkv_ref[:, j * dqkv + g * LANES:j * dqkv + (g + 1) * LANES]
        _qkv_finish(y, g, q_ref, k_ref, v_ref)

    o_z = o_qkv + dqkv
    z_ref[...] = _dot(hb, w_ref[:, o_z:o_z + d]).astype(BF16)
    ga_ref[...] = _sigmoid(_dot(hb, w_ref[:, o_z + d:o_z + 2 * d])).astype(BF16)
    gb_ref[...] = _sigmoid(_dot(hb, w_ref[:, o_z + 2 * d:o_z + 3 * d])).astype(BF16)
    ba = _dot(hb, w_ref[:, o_z + 3 * d:o_z + 3 * d + LANES])
    bg_ref[...] = _beta_decay(ba, alog_ref[...], dtb_ref[...])


def _pre_sample_call(x, mod_s, n1w, w_pack, cw, alog_p, dtb_p, sconf, sqkv, dww, dwb):
    n, d = x.shape
    dqkv = cw.shape[1]
    dc = dwb.shape[1]
    tm = 32
    assert n % tm == 0
    tok = lambda w: pl.BlockSpec((tm, w), lambda i: (i, 0))
    kern = functools.partial(_pre_sample_kernel, d=d, dc=dc, dqkv=dqkv)
    return pl.pallas_call(
        kern,
        grid=(n // tm,),
        in_specs=[tok(d), tok(mod_s.shape[1]),
                  _resident(n1w.shape), _resident(w_pack.shape), _resident(cw.shape),
                  _resident(alog_p.shape), _resident(dtb_p.shape),
                  tok(sconf.shape[1]), tok(sqkv.shape[1]),
                  _resident(dww.shape), _resident(dwb.shape)],
        out_specs=[tok(dc), tok(d), tok(d), tok(d), tok(d), tok(d), tok(d), tok(LANES),
                   tok(sconf.shape[1]), tok(sqkv.shape[1])],
        out_shape=[jax.ShapeDtypeStruct((n, dc), F32),
                   jax.ShapeDtypeStruct((n, d), F32), jax.ShapeDtypeStruct((n, d), F32),
                   jax.ShapeDtypeStruct((n, d), F32), jax.ShapeDtypeStruct((n, d), BF16),
                   jax.ShapeDtypeStruct((n, d), BF16), jax.ShapeDtypeStruct((n, d), BF16),
                   jax.ShapeDtypeStruct((n, LANES), F32),
                   jax.ShapeDtypeStruct(sconf.shape, F32), jax.ShapeDtypeStruct(sqkv.shape, F32)],
        compiler_params=_params(1),
        name="pre_sample",
    )(x, mod_s, n1w, w_pack, cw, alog_p, dtb_p, sconf, sqkv, dww, dwb)


def _half_block_mask(row, col, size):
    shift = size.bit_length() - 1
    half = size // 2
    return ((lax.shift_right_logical(row, shift) == lax.shift_right_logical(col, shift))
            & ((row & half) != 0) & ((col & half) == 0))


def _block_diag2(a, b):
    z = jnp.zeros(a.shape, a.dtype)
    return jnp.concatenate([jnp.concatenate([a, z], axis=1), jnp.concatenate([z, b], axis=1)], axis=0)


def _pair_dot(a, b):
    n = b[0].shape[1]
    out = _dot(jnp.concatenate(a, axis=1), _block_diag2(b[0], b[1]))
    return [out[:, :n], out[:, n:]]


def _gdn_prompt_kernel(q_ref, k_ref, v_ref, bg_ref, bgt_ref, o_ref, sfin_ref, s_scr, *, w, nw, n_steps):
    step = pl.program_id(1)

    @pl.when(step == 0)
    def _():
        s_scr[...] = jnp.zeros(s_scr.shape, F32)

    nc = w // CHUNK
    heads = range(N_HEADS)
    units = [(win, h) for win in range(nw) for h in heads]
    n_units = len(units)
    upairs = range(0, n_units, 2)
    row = lax.broadcasted_iota(jnp.int32, (w, w), 0)
    col = lax.broadcasted_iota(jnp.int32, (w, w), 1)
    shift = CHUNK.bit_length() - 1
    same = lax.shift_right_logical(row, shift) == lax.shift_right_logical(col, shift)
    m_incl = same & (col <= row)
    m_strict = same & (col < row)
    eye = (row == col).astype(F32)

    rs = [slice(win * w, (win + 1) * w) for win, _ in units]
    hs = [slice(h * HEAD_DIM, (h + 1) * HEAD_DIM) for _, h in units]
    qh = [q_ref[rs[u], hs[u]] for u in range(n_units)]
    kh = [k_ref[rs[u], hs[u]] for u in range(n_units)]
    vh = [v_ref[rs[u], hs[u]] for u in range(n_units)]
    bcol, gcb, decay = [], [], []
    for u, (win, h) in enumerate(units):
        bcol.append(jnp.broadcast_to(bg_ref[rs[u], h:h + 1], (w, HEAD_DIM)))
        gcb.append(jnp.broadcast_to(bg_ref[rs[u], N_HEADS + h:N_HEADS + h + 1], (w, HEAD_DIM)))
        grow = bgt_ref[N_HEADS + h:N_HEADS + h + 1, rs[u]]
        decay.append(jnp.exp(jnp.where(m_incl, gcb[u] - grow, -1e30)))

    kb = [kh[u].astype(BF16) for u in range(n_units)]
    qb = [qh[u].astype(BF16) for u in range(n_units)]
    amat, att = [None] * n_units, [None] * n_units
    for p in upairs:
        lhs = jnp.concatenate([jnp.concatenate([kb[p], kb[p + 1]], axis=1),
                               jnp.concatenate([qb[p], qb[p + 1]], axis=1)], axis=0)
        r = _dot_nt(lhs, _block_diag2(kb[p], kb[p + 1]))
        for j in range(2):
            kk = r[:w, j * w:(j + 1) * w]
            amat[p + j] = jnp.where(m_strict, kk * bcol[p + j] * decay[p + j], 0.0)
            att[p + j] = (r[w:, j * w:(j + 1) * w] * decay[p + j]).astype(BF16)

    x = [eye - jnp.where(_half_block_mask(row, col, 2), amat[u], 0.0) for u in range(n_units)]
    size = 4
    while size <= CHUNK:
        mask = _half_block_mask(row, col, size)
        for p in upairs:
            blk = [jnp.where(mask, amat[p + j], 0.0).astype(BF16) for j in range(2)]
            xb = [x[p + j].astype(BF16) for j in range(2)]
            y = _pair_dot(blk, xb)
            upd = _pair_dot(xb, [y[j].astype(BF16) for j in range(2)])
            for j in range(2):
                x[p + j] = x[p + j] - upd[j]
        size *= 2

    eg = [jnp.exp(gcb[u]) for u in range(n_units)]
    kv = []
    for u in range(n_units):
        rhs = jnp.concatenate([kh[u] * (bcol[u] * eg[u]), vh[u] * bcol[u]], axis=1).astype(BF16)
        kv.append(_dot(x[u].astype(BF16), rhs).astype(BF16))
    qe, glast, ktt = [], [], []
    for u in range(n_units):
        qe.append(qh[u] * eg[u])
        gl = jnp.concatenate(
            [jnp.broadcast_to(gcb[u][(c + 1) * CHUNK - 1:(c + 1) * CHUNK, :], (CHUNK, HEAD_DIM)) for c in range(nc)],
            axis=0)
        glast.append(gl)
        ktt.append((kh[u] * jnp.exp(gl - gcb[u])).T.astype(BF16))
    wmat = [[None] * nc for _ in range(n_units)]
    bmat = [[None] * nc for _ in range(n_units)]
    qmat = [[None] * nc for _ in range(n_units)]
    omat = [[None] * nc for _ in range(n_units)]
    zeros = jnp.zeros((CHUNK, 2 * HEAD_DIM), BF16)
    for c in range(nc):
        cs = slice(c * CHUNK, (c + 1) * CHUNK)
        for u in range(n_units):
            parts = [zeros] * nc
            parts[c] = kv[u][cs]
            kv_c = jnp.concatenate(parts, axis=0)
            r = _dot(jnp.concatenate([ktt[u], att[u][cs]], axis=0), kv_c)
            wmat[u][c] = r[:HEAD_DIM, :HEAD_DIM]
            bmat[u][c] = r[:HEAD_DIM, HEAD_DIM:]
            qmat[u][c] = qe[u][cs] - r[HEAD_DIM:, :HEAD_DIM]
            omat[u][c] = r[HEAD_DIM:, HEAD_DIM:]

    s = [s_scr[h] for h in heads]
    for win in range(nw):
        for c in range(nc):
            cs = slice(win * w + c * CHUNK, win * w + (c + 1) * CHUNK)
            for hp in range(0, N_HEADS, 2):
                u = win * N_HEADS + hp
                lhs = [jnp.concatenate([wmat[u + j][c], qmat[u + j][c]], axis=0).astype(BF16) for j in range(2)]
                r = _pair_dot(lhs, [s[hp + j].astype(BF16) for j in range(2)])
                for j in range(2):
                    o_ref[cs, hs[u + j]] = r[j][HEAD_DIM:] + omat[u + j][c]
                    decay_c = jnp.exp(glast[u + j][c * CHUNK:c * CHUNK + 1, :])
                    s[hp + j] = s[hp + j] * decay_c - r[j][:HEAD_DIM] + bmat[u + j][c]
    for h in heads:
        s_scr[h] = s[h]

    @pl.when(step == n_steps - 1)
    def _():
        sfin_ref[0] = s_scr[...]


def _gdn_prompt_call(qn, kn, v, bg, bgt, nb, seq_len):
    t, d = qn.shape
    w = GDN_WINDOW
    nw = GDN_WINDOWS_PER_STEP
    assert seq_len % (w * nw) == 0 and w % CHUNK == 0 and w == HEAD_DIM
    n_steps = seq_len // (w * nw)
    tok = pl.BlockSpec((w * nw, d), lambda b, i: (b * n_steps + i, 0))
    kern = functools.partial(_gdn_prompt_kernel, w=w, nw=nw, n_steps=n_steps)
    return pl.pallas_call(
        kern,
        grid=(nb, n_steps),
        in_specs=[tok, tok, tok,
                  pl.BlockSpec((w * nw, LANES), lambda b, i: (b * n_steps + i, 0)),
                  pl.BlockSpec((LANES, w * nw), lambda b, i: (0, b * n_steps + i))],
        out_specs=[tok, pl.BlockSpec((1, N_HEADS, HEAD_DIM, HEAD_DIM), lambda b, i: (b, 0, 0, 0))],
        out_shape=[jax.ShapeDtypeStruct((t, d), F32),
                   jax.ShapeDtypeStruct((nb, N_HEADS, HEAD_DIM, HEAD_DIM), F32)],
        scratch_shapes=[pltpu.VMEM((N_HEADS, HEAD_DIM, HEAD_DIM), F32)],
        compiler_params=_params(2),
        name="gdn_prompt",
    )(qn, kn, v, bg, bgt)


def _gdn_sample_kernel(q_ref, k_ref, v_ref, bg_ref, s_ref, o_ref, snew_ref, *, bs):
    row = lax.broadcasted_iota(jnp.int32, (HEAD_DIM, HEAD_DIM), 0)
    col = lax.broadcasted_iota(jnp.int32, (HEAD_DIM, HEAD_DIM), 1)
    eye = row == col
    for h in range(N_HEADS):
        hs = slice(h * HEAD_DIM, (h + 1) * HEAD_DIM)
        qh = q_ref[:, hs]
        kh = k_ref[:, hs]
        vh = v_ref[:, hs]
        beta = jnp.broadcast_to(bg_ref[:, h:h + 1], (bs, HEAD_DIM))
        eg = jnp.exp(jnp.broadcast_to(bg_ref[:, N_HEADS + h:N_HEADS + h + 1], (bs, HEAD_DIM)))
        att = jnp.sum(qh * kh, axis=-1, keepdims=True)
        for i in range(bs):
            s = s_ref[i, h]
            kcol = jnp.sum(jnp.where(eye, jnp.broadcast_to(kh[i:i + 1, :], (HEAD_DIM, HEAD_DIM)), 0.0),
                           axis=-1, keepdims=True)
            qcol = jnp.sum(jnp.where(eye, jnp.broadcast_to(qh[i:i + 1, :], (HEAD_DIM, HEAD_DIM)), 0.0),
                           axis=-1, keepdims=True)
            ks = jnp.sum(kcol * s, axis=0, keepdims=True)
            qs = jnp.sum(qcol * s, axis=0, keepdims=True)
            egi = eg[i:i + 1, :]
            v_new = beta[i:i + 1, :] * (vh[i:i + 1, :] - egi * ks)
            o_ref[i:i + 1, hs] = egi * qs + att[i:i + 1, :] * v_new
            snew_ref[i, h] = s * egi + kcol * v_new


def _gdn_sample_call(qn, kn, v, bg, s0):
    n, d = qn.shape
    bs = SAMPLE_SEQ_BLOCK
    assert n % bs == 0
    tok = lambda w: pl.BlockSpec((bs, w), lambda i: (i, 0))
    st = pl.BlockSpec((bs, N_HEADS, HEAD_DIM, HEAD_DIM), lambda i: (i, 0, 0, 0))
    return pl.pallas_call(
        functools.partial(_gdn_sample_kernel, bs=bs),
        grid=(n // bs,),
        in_specs=[tok(d), tok(d), tok(d), tok(LANES), st],
        out_specs=[tok(d), st],
        out_shape=[jax.ShapeDtypeStruct((n, d), F32), jax.ShapeDtypeStruct(s0.shape, F32)],
        compiler_params=_params(1),
        name="gdn_sample",
    )(qn, kn, v, bg, s0)


def _rms(x, w):
    return x * lax.rsqrt(jnp.mean(x * x, axis=-1, keepdims=True) + EPS) * w


def _ln_swish(a, lnw_ref, lnb_ref):
    mu = jnp.mean(a, axis=-1, keepdims=True)
    ac = a - mu
    var = jnp.mean(ac * ac, axis=-1, keepdims=True)
    return _silu(ac * lax.rsqrt(var + EPS) * lnw_ref[...] + lnb_ref[...]).astype(BF16)


def _conv_ln_swish_jobs(ext_scr, sh_scr, cv_scr, a_scr, dww_ref, dwb_ref, lnw_ref, lnb_ref, *, tm, dc):
    n_taps = dww_ref.shape[0]
    rows = tm + CONV_HALO - SUBLANES

    def shifts():
        for r in range(1, SUBLANES):
            sh_scr[r - 1] = ext_scr[r:r + rows, :]

    def taps(g):
        cols = slice(g * LANES, (g + 1) * LANES)
        acc = None
        for j in range(n_taps):
            off = CONV_HALO - (n_taps - 1) + j
            r, base = off % SUBLANES, off - off % SUBLANES
            src = ext_scr[base:base + tm, cols] if r == 0 else sh_scr[r - 1, base:base + tm, cols]
            term = dww_ref[j:j + 1, cols] * src
            acc = term if acc is None else acc + term
        cv_scr[:, cols] = acc + dwb_ref[:, cols]

    def finish():
        a_scr[...] = _ln_swish(cv_scr[...], lnw_ref, lnb_ref)

    return [shifts] + [functools.partial(taps, g) for g in range(dc // LANES)] + [finish]


def _post_tail(a, x_ref, mod, o_ref, z_ref, gnw_ref, ga_ref, gb_ref, wca_ref, wgo_ref, wo_ref, n2w_ref,
               w1_ref, w2_ref, fnw_ref, y_ref, *, dff, side_jobs=()):
    side = list(side_jobs)

    def run_side():
        if side:
            side.pop(0)()

    y_a = _dot(a, wca_ref[...])
    run_side()

    gated = []
    for h in range(N_HEADS):
        hs = slice(h * HEAD_DIM, (h + 1) * HEAD_DIM)
        gated.append(_rms(o_ref[:, hs], gnw_ref[...]) * _silu(z_ref[:, hs].astype(F32)))
    y_b = _dot(jnp.concatenate(gated, axis=1).astype(BF16), wgo_ref[...])
    run_side()

    merged = ga_ref[...].astype(F32) * y_a + gb_ref[...].astype(F32) * y_b
    x1 = x_ref[...] + mod(2) * _dot(merged.astype(BF16), wo_ref[...])
    run_side()

    h2 = (_rms(x1, n2w_ref[...]) * (1.0 + mod(4)) + mod(3)).astype(BF16)
    step = 1024
    ff = None
    for c in range(dff // step):
        f = jnp.maximum(_dot(h2, w1_ref[:, c * step:(c + 1) * step]), 0.0)
        part = _dot((f * f).astype(BF16), w2_ref[c * step:(c + 1) * step, :])
        ff = part if ff is None else ff + part
        run_side()
    while side:
        run_side()
    x2 = x1 + mod(5) * ff
    y_ref[...] = _rms(x2, fnw_ref[...])


def _post_prompt_kernel(x_ref, mod_ref, glu_ref, glun_ref, dww_ref, dwb_ref, lnw_ref, lnb_ref,
                        o_ref, z_ref, gnw_ref, ga_ref, gb_ref, wca_ref, wgo_ref, wo_ref, n2w_ref,
                        w1_ref, w2_ref, fnw_ref, y_ref, ext_scr, sh_scr, cv_scr, a_scr,
                        *, tm, tiles_per_seq, d, dc, dff):
    i = pl.program_id(0)
    conv_jobs = functools.partial(_conv_ln_swish_jobs, ext_scr, sh_scr, cv_scr, a_scr, dww_ref, dwb_ref,
                                  lnw_ref, lnb_ref, tm=tm, dc=dc)

    @pl.when(i == 0)
    def _():
        ext_scr[0:CONV_HALO, :] = jnp.zeros((CONV_HALO, dc), F32)
        ext_scr[CONV_HALO:CONV_HALO + tm, :] = glu_ref[...]
        for job in conv_jobs():
            job()

    a = a_scr[...]
    ext_scr[0:CONV_HALO, :] = jnp.where((i + 1) % tiles_per_seq == 0, 0.0, glu_ref[tm - CONV_HALO:tm, :])
    ext_scr[CONV_HALO:CONV_HALO + tm, :] = glun_ref[...]

    mod = lambda k: mod_ref[0, :, k * d:(k + 1) * d]
    _post_tail(a, x_ref, mod, o_ref, z_ref, gnw_ref, ga_ref, gb_ref, wca_ref, wgo_ref, wo_ref, n2w_ref,
               w1_ref, w2_ref, fnw_ref, y_ref, dff=dff, side_jobs=conv_jobs())


def _post_sample_kernel(x_ref, mod_ref, apre_ref, lnw_ref, lnb_ref,
                        o_ref, z_ref, gnw_ref, ga_ref, gb_ref, wca_ref, wgo_ref, wo_ref, n2w_ref,
                        w1_ref, w2_ref, fnw_ref, y_ref, *, d, dff):
    a = _ln_swish(apre_ref[...], lnw_ref, lnb_ref)
    mod = lambda k: mod_ref[:, k * d:(k + 1) * d]
    _post_tail(a, x_ref, mod, o_ref, z_ref, gnw_ref, ga_ref, gb_ref, wca_ref, wgo_ref, wo_ref, n2w_ref,
               w1_ref, w2_ref, fnw_ref, y_ref, dff=dff)


def _post_call(x, mod, conv_in, lnw, lnb, o, z, gnw, ga, gb, wca, wgo, wo, n2w, w1, w2, fnw, *, seq_len):
    t, d = x.shape
    dc = lnw.shape[1]
    dff = w1.shape[1]
    prompt = len(conv_in) == 3
    tm = min(POST_TILE, t)
    assert t % tm == 0
    n_tiles = t // tm
    tok = lambda w: pl.BlockSpec((tm, w), lambda i: (i, 0))
    common_args = [lnw, lnb, o, z, gnw, ga, gb, wca, wgo, wo, n2w, w1, w2, fnw]
    common_specs = [_resident(lnw.shape), _resident(lnb.shape), tok(d), tok(d), _resident(gnw.shape), tok(d), tok(d),
                    _resident(wca.shape), _resident(wgo.shape), _resident(wo.shape), _resident(n2w.shape),
                    _resident(w1.shape), _resident(w2.shape), _resident(fnw.shape)]
    if prompt:
        glu, dww, dwb = conv_in
        assert seq_len % tm == 0 and tm >= CONV_HALO >= dww.shape[0] - 1
        tiles_per_seq = seq_len // tm
        args = [x, mod, glu, glu, dww, dwb] + common_args
        specs = [tok(d), pl.BlockSpec((1, 1, mod.shape[2]), lambda i: (i // tiles_per_seq, 0, 0)),
                 tok(dc), pl.BlockSpec((tm, dc), lambda i: (jnp.minimum(i + 1, n_tiles - 1), 0)),
                 _resident(dww.shape), _resident(dwb.shape)] + common_specs
        scratch = [pltpu.VMEM((tm + CONV_HALO, dc), F32),
                   pltpu.VMEM((SUBLANES - 1, tm + CONV_HALO - SUBLANES, dc), F32),
                   pltpu.VMEM((tm, dc), F32),
                   pltpu.VMEM((tm, dc), BF16)]
        kern = functools.partial(_post_prompt_kernel, tm=tm, tiles_per_seq=tiles_per_seq, d=d, dc=dc, dff=dff)
        name = "post_prompt"
    else:
        args = [x, mod, conv_in[0]] + common_args
        specs = [tok(d), tok(mod.shape[1]), tok(dc)] + common_specs
        scratch = []
        kern = functools.partial(_post_sample_kernel, d=d, dff=dff)
        name = "post_sample"
    return pl.pallas_call(
        kern,
        grid=(n_tiles,),
        in_specs=specs,
        out_specs=tok(d),
        out_shape=jax.ShapeDtypeStruct((t, d), F32),
        scratch_shapes=scratch,
        compiler_params=_params(1),
        name=name,
    )(*args)


def _pack_w_in(w_in, d, dc, dqkv):
    o1 = 2 * dc + dqkv + d
    ba = w_in[:, o1:o1 + 2 * N_HEADS]
    gates = w_in[:, o1 + 2 * N_HEADS:]
    pad = jnp.zeros((w_in.shape[0], LANES - 2 * N_HEADS), w_in.dtype)
    return jnp.concatenate([w_in[:, :o1], gates, ba, pad], axis=1).astype(BF16)


def _lane_pad(vec, offset):
    out = jnp.zeros((1, LANES), F32)
    return out.at[0, offset:offset + vec.shape[0]].set(vec.astype(F32))


def kernel(x_prompt, x_sample, c_prompt, c_sample, state_conf_conv, state_qkv_conv, state_delta, w_ada, b_ada, norm1_w, w_in, conf_dw_w, conf_dw_b, conf_ln_w, conf_ln_b, w_conf_out, gdn_conv_w, a_log, dt_bias, gdn_norm_w, w_gdn_out, w_o, norm2_w, w_ff1, w_ff2, final_norm_w):
    nb, seq_len, d = x_prompt.shape
    ns = x_sample.shape[0]
    depth = w_ada.shape[0]
    dc = conf_dw_w.shape[2]
    dqkv = gdn_conv_w.shape[2]
    n_hist = conf_dw_w.shape[1] - 1
    n_taps = gdn_conv_w.shape[1]
    assert x_sample.shape[1] == 1

    xp = x_prompt.reshape(nb * seq_len, d)
    xs = x_sample.reshape(ns, d)
    c_all = jnp.concatenate([c_prompt, c_sample], axis=0)
    row2 = lambda v: v.reshape(1, -1).astype(F32)

    conf_p, qkv_p, delta_p, conf_s, qkv_s, delta_s = [], [], [], [], [], []
    for l in range(depth):
        mod = _mod_call(c_all, w_ada[l].astype(BF16), row2(b_ada[l]))
        mod_p = mod[:nb].reshape(nb, 1, N_MOD * d)
        mod_s = mod[nb:]
        w_pack = _pack_w_in(w_in[l], d, dc, dqkv)
        n1w, n2w, fnw = row2(norm1_w[l]), row2(norm2_w[l]), row2(final_norm_w)
        alog_p = _lane_pad(a_log[l], N_HEADS)
        dtb_p = _lane_pad(dt_bias[l], N_HEADS)
        cw = gdn_conv_w[l].astype(F32)
        dww, dwb = conf_dw_w[l].astype(F32), row2(conf_dw_b[l])
        lnw, lnb = row2(conf_ln_w[l]), row2(conf_ln_b[l])
        gnw = row2(gdn_norm_w[l])
        wca, wgo, wo = w_conf_out[l].astype(BF16), w_gdn_out[l].astype(BF16), w_o[l].astype(BF16)
        w1, w2 = w_ff1[l].astype(BF16), w_ff2[l].astype(BF16)
        assert depth == 1

        glu, qn, kn, v, z, ga, gb, bg, bgt, tail = _pre_prompt_call(
            xp, mod_p, n1w, w_pack, cw, alog_p, dtb_p, seq_len)
        o, s_fin = _gdn_prompt_call(qn, kn, v, bg, bgt, nb, seq_len)
        xp = _post_call(xp, mod_p, (glu, dww, dwb), lnw, lnb, o, z, gnw, ga, gb, wca, wgo, wo, n2w, w1, w2, fnw,
                        seq_len=seq_len)
        conf_p.append(glu.reshape(nb, seq_len, dc)[:, seq_len - n_hist:])
        qkv_p.append(tail.reshape(nb, SUBLANES, dqkv)[:, SUBLANES - (n_taps - 1):])
        delta_p.append(s_fin)

        sconf = state_conf_conv[l].reshape(ns, n_hist * dc)
        sqkv = state_qkv_conv[l].reshape(ns, (n_taps - 1) * dqkv)
        apre, qn, kn, v, z, ga, gb, bg, nconf, nqkv = _pre_sample_call(
            xs, mod_s, n1w, w_pack, cw, alog_p, dtb_p, sconf, sqkv, dww, dwb)
        o, s_new = _gdn_sample_call(qn, kn, v, bg, state_delta[l].astype(F32))
        xs = _post_call(xs, mod_s, (apre,), lnw, lnb, o, z, gnw, ga, gb, wca, wgo, wo, n2w, w1, w2, fnw,
                        seq_len=1)
        conf_s.append(nconf.reshape(ns, n_hist, dc))
        qkv_s.append(nqkv.reshape(ns, n_taps - 1, dqkv))
        delta_s.append(s_new)

    return (xp.reshape(nb, seq_len, d), xs.reshape(ns, 1, d),
            jnp.stack(conf_p), jnp.stack(qkv_p), jnp.stack(delta_p),
            jnp.stack(conf_s), jnp.stack(qkv_s), jnp.stack(delta_s))
```

```python
import functools

import jax
import jax.numpy as jnp
from jax import lax
from jax.experimental import pallas as pl
from jax.experimental.pallas import tpu as pltpu

F32 = jnp.float32
BF16 = jnp.bfloat16

EPS = 1e-6
N_MOD = 6
N_HEADS = 8
HEAD_DIM = 128
CHUNK = 64
LANES = 128
SUBLANES = 8
VMEM_LIMIT_BYTES = 56 * 1024 * 1024

PRE_TILE = 256
POST_TILE = 256
GDN_WINDOW = 128
GDN_WINDOWS_PER_STEP = 4
CONV_HALO = 32
SAMPLE_SEQ_BLOCK = 8


NEG_LOG2_E = -1.4426950408889634


def _sigmoid(x):
    return 1.0 / (1.0 + jnp.exp2(x * NEG_LOG2_E))


def _silu(x):
    return x * _sigmoid(x)


def _softplus(x):
    return jnp.maximum(x, 0.0) + jnp.log(1.0 + jnp.exp(-jnp.abs(x)))


def _dot(a, b):
    return jnp.dot(a, b, preferred_element_type=F32)


def _dot_nt(a, b):
    return lax.dot_general(a, b, (((1,), (1,)), ((), ())), preferred_element_type=F32)


def _resident(shape):
    n = len(shape)
    return pl.BlockSpec(shape, lambda *_: (0,) * n, pipeline_mode=pl.Buffered(1))


def _params(n_grid):
    return pltpu.CompilerParams(dimension_semantics=("arbitrary",) * n_grid,
                                vmem_limit_bytes=VMEM_LIMIT_BYTES)


def _mod_kernel(c_ref, w_ref, b_ref, o_ref):
    c = c_ref[...]
    o_ref[...] = _dot(_silu(c).astype(BF16), w_ref[...].astype(BF16)) + b_ref[...]


def _mod_call(c_all, w_ada, b_ada):
    n, d = c_all.shape
    e = w_ada.shape[1]
    tn = e // 8
    return pl.pallas_call(
        _mod_kernel,
        grid=(e // tn,),
        in_specs=[pl.BlockSpec((n, d), lambda j: (0, 0)),
                  pl.BlockSpec((d, tn), lambda j: (0, j)),
                  pl.BlockSpec((1, tn), lambda j: (0, j))],
        out_specs=pl.BlockSpec((n, tn), lambda j: (0, j)),
        out_shape=jax.ShapeDtypeStruct((n, e), F32),
        compiler_params=_params(1),
        name="mod",
    )(c_all, w_ada, b_ada)


def _pre_front(x, shift1, scale1, n1w):
    ms = jnp.mean(x * x, axis=-1, keepdims=True)
    h = x * lax.rsqrt(ms + EPS) * n1w
    return (h * (1.0 + scale1) + shift1).astype(BF16)


def _qkv_finish(y, g, q_ref, k_ref, v_ref):
    s = _silu(y)
    d = q_ref.shape[1]
    n_qk = d // LANES
    if g < 2 * n_qk:
        s = s * lax.rsqrt(jnp.sum(s * s, axis=-1, keepdims=True) + EPS)
    if g < n_qk:
        q_ref[:, g * LANES:(g + 1) * LANES] = s * (HEAD_DIM ** -0.5)
    elif g < 2 * n_qk:
        k_ref[:, (g - n_qk) * LANES:(g - n_qk + 1) * LANES] = s
    else:
        v_ref[:, (g - 2 * n_qk) * LANES:(g - 2 * n_qk + 1) * LANES] = s


def _beta_decay(ba, alog, dtb):
    lane = lax.broadcasted_iota(jnp.int32, ba.shape, 1)
    beta = _sigmoid(ba)
    g = -jnp.exp(alog) * _softplus(ba + dtb)
    return jnp.where(lane < N_HEADS, beta, g)


def _pre_prompt_kernel(x_ref, mod_ref, n1w_ref, w_ref, wt_ref, cw_ref, alog_ref, dtb_ref, cum_ref,
                       glu_ref, q_ref, k_ref, v_ref, z_ref, ga_ref, gb_ref, bg_ref, bgt_ref, tail_ref,
                       qkv_scr, h_scr, *, tm, tiles_per_seq, d, dc, dqkv):
    i = pl.program_id(0)
    h_scr[...] = _pre_front(x_ref[...], mod_ref[0, :, 0:d], mod_ref[0, :, d:2 * d], n1w_ref[...])

    u = _dot(h_scr[...], w_ref[:, 0:2 * dc])
    glu_ref[...] = u[:, :dc] * _sigmoid(u[:, dc:])

    @pl.when(i % tiles_per_seq == 0)
    def _():
        qkv_scr[0:SUBLANES, :] = jnp.zeros((SUBLANES, dqkv), F32)

    o_qkv = 2 * dc
    o_z = o_qkv + dqkv
    step = 512
    n_chunks = dqkv // step
    n_taps = cw_ref.shape[0]

    def project(c):
        cols = slice(c * step, (c + 1) * step)
        qkv_scr[SUBLANES:SUBLANES + tm, cols] = _dot(h_scr[...], w_ref[:, o_qkv + c * step:o_qkv + (c + 1) * step])
        tail_ref[:, cols] = qkv_scr[tm:tm + SUBLANES, cols]

    def conv(c):
        for g in range(c * step // LANES, (c + 1) * step // LANES):
            cols = slice(g * LANES, (g + 1) * LANES)
            y = None
            for j in range(n_taps):
                r0 = SUBLANES - (n_taps - 1) + j
                term = cw_ref[j:j + 1, cols] * qkv_scr[r0:r0 + tm, cols]
                y = term if y is None else y + term
            _qkv_finish(y, g, q_ref, k_ref, v_ref)
        cols = slice(c * step, (c + 1) * step)
        qkv_scr[0:SUBLANES, cols] = qkv_scr[tm:tm + SUBLANES, cols]

    pieces = [(ref, wsrc, o_w + half * step, half * step, act)
              for ref, wsrc, o_w, act in [(z_ref, w_ref, o_z, False), (ga_ref, wt_ref, 0, True), (gb_ref, wt_ref, d, True)]
              for half in range(d // step)]
    assert len(pieces) == n_chunks

    def side(c):
        ref, wsrc, o_w, o_out, act = pieces[c]
        r = _dot(h_scr[...], wsrc[:, o_w:o_w + step])
        ref[:, o_out:o_out + step] = (_sigmoid(r) if act else r).astype(BF16)

    project(0)
    for c in range(n_chunks):
        if c + 1 < n_chunks:
            project(c + 1)
        side(c)
        conv(c)

    ba = _dot(h_scr[...], wt_ref[:, 2 * d:2 * d + LANES])
    bg = _beta_decay(ba, alog_ref[...], dtb_ref[...])
    cum = jnp.dot(cum_ref[...], bg, preferred_element_type=F32, precision=lax.Precision.HIGHEST)
    lane = lax.broadcasted_iota(jnp.int32, bg.shape, 1)
    bg = jnp.where(lane < N_HEADS, bg, cum)
    bg_ref[...] = bg
    bgt_ref[...] = bg.T


def _pre_prompt_call(x, mod_p, n1w, w_main, w_tail, cw, alog_p, dtb_p, seq_len):
    t, d = x.shape
    tm = PRE_TILE
    assert seq_len % tm == 0 and tm % CHUNK == 0
    dqkv = cw.shape[1]
    dc = (w_main.shape[1] - dqkv - d) // 2
    nb = t // seq_len
    tiles_per_seq = seq_len // tm
    row = lax.broadcasted_iota(jnp.int32, (tm, tm), 0)
    col = lax.broadcasted_iota(jnp.int32, (tm, tm), 1)
    cum_mat = ((row // CHUNK == col // CHUNK) & (col <= row)).astype(F32)

    tok = lambda w: pl.BlockSpec((tm, w), lambda i: (i, 0))
    kern = functools.partial(_pre_prompt_kernel, tm=tm, tiles_per_seq=tiles_per_seq, d=d, dc=dc, dqkv=dqkv)
    return pl.pallas_call(
        kern,
        grid=(t // tm,),
        in_specs=[tok(d),
                  pl.BlockSpec((1, 1, mod_p.shape[2]), lambda i: (i // tiles_per_seq, 0, 0)),
                  _resident(n1w.shape), _resident(w_main.shape), _resident(w_tail.shape), _resident(cw.shape),
                  _resident(alog_p.shape), _resident(dtb_p.shape), _resident(cum_mat.shape)],
        out_specs=[tok(dc), tok(d), tok(d), tok(d), tok(d), tok(d), tok(d), tok(LANES),
                   pl.BlockSpec((LANES, tm), lambda i: (0, i)),
                   pl.BlockSpec((SUBLANES, dqkv), lambda i: (i // tiles_per_seq, 0))],
        out_shape=[jax.ShapeDtypeStruct((t, dc), F32),
                   jax.ShapeDtypeStruct((t, d), F32), jax.ShapeDtypeStruct((t, d), F32),
                   jax.ShapeDtypeStruct((t, d), F32), jax.ShapeDtypeStruct((t, d), BF16),
                   jax.ShapeDtypeStruct((t, d), BF16), jax.ShapeDtypeStruct((t, d), BF16),
                   jax.ShapeDtypeStruct((t, LANES), F32),
                   jax.ShapeDtypeStruct((LANES, t), F32),
                   jax.ShapeDtypeStruct((nb * SUBLANES, dqkv), F32)],
        scratch_shapes=[pltpu.VMEM((tm + SUBLANES, dqkv), F32), pltpu.VMEM((tm, d), BF16)],
        compiler_params=_params(1),
        name="pre_prompt",
    )(x, mod_p, n1w, w_main, w_tail, cw, alog_p, dtb_p, cum_mat)


def _pre_sample_kernel(x_ref, mod_ref, n1w_ref, w_ref, wt_ref, cw_ref, alog_ref, dtb_ref,
                       sconf_ref, sqkv_ref, dww_ref, dwb_ref,
                       apre_ref, q_ref, k_ref, v_ref, z_ref, ga_ref, gb_ref, bg_ref,
                       nconf_ref, nqkv_ref, *, d, dc, dqkv):
    hb = _pre_front(x_ref[...], mod_ref[:, 0:d], mod_ref[:, d:2 * d], n1w_ref[...])

    u = _dot(hb, w_ref[:, 0:2 * dc])
    glu = u[:, :dc] * _sigmoid(u[:, dc:])
    n_hist = sconf_ref.shape[1] // dc
    acc = dww_ref[n_hist:n_hist + 1, :] * glu + dwb_ref[...]
    for j in range(n_hist):
        acc = acc + dww_ref[j:j + 1, :] * sconf_ref[:, j * dc:(j + 1) * dc]
    apre_ref[...] = acc
    nconf_ref[:, 0:(n_hist - 1) * dc] = sconf_ref[:, dc:n_hist * dc]
    nconf_ref[:, (n_hist - 1) * dc:n_hist * dc] = glu

    o_qkv = 2 * dc
    n_taps = cw_ref.shape[0]
    nqkv_ref[:, 0:(n_taps - 2) * dqkv] = sqkv_ref[:, dqkv:(n_taps - 1) * dqkv]
    for g in range(dqkv // LANES):
        cols = slice(g * LANES, (g + 1) * LANES)
        raw = _dot(hb, w_ref[:, o_qkv + g * LANES:o_qkv + (g + 1) * LANES])
        nqkv_ref[:, (n_taps - 2) * dqkv + g * LANES:(n_taps - 2) * dqkv + (g + 1) * LANES] = raw
        y = cw_ref[n_taps - 1:n_taps, cols] * raw
        for j in range(n_taps - 1):
            y = y + cw_ref[j:j + 1, cols] * sqkv_ref[:, j * dqkv + g * LANES:j * dqkv + (g + 1) * LANES]
        _qkv_finish(y, g, q_ref, k_ref, v_ref)

    o_z = o_qkv + dqkv
    z_ref[...] = _dot(hb, w_ref[:, o_z:o_z + d]).astype(BF16)
    ga_ref[...] = _sigmoid(_dot(hb, wt_ref[:, 0:d])).astype(BF16)
    gb_ref[...] = _sigmoid(_dot(hb, wt_ref[:, d:2 * d])).astype(BF16)
    ba = _dot(hb, wt_ref[:, 2 * d:2 * d + LANES])
    bg_ref[...] = _beta_decay(ba, alog_ref[...], dtb_ref[...])


def _pre_sample_call(x, mod_s, n1w, w_main, w_tail, cw, alog_p, dtb_p, sconf, sqkv, dww, dwb):
    n, d = x.shape
    dqkv = cw.shape[1]
    dc = dwb.shape[1]
    tm = 32
    assert n % tm == 0
    tok = lambda w: pl.BlockSpec((tm, w), lambda i: (i, 0))
    kern = functools.partial(_pre_sample_kernel, d=d, dc=dc, dqkv=dqkv)
    return pl.pallas_call(
        kern,
        grid=(n // tm,),
        in_specs=[tok(d), tok(mod_s.shape[1]),
                  _resident(n1w.shape), _resident(w_main.shape), _resident(w_tail.shape), _resident(cw.shape),
                  _resident(alog_p.shape), _resident(dtb_p.shape),
                  tok(sconf.shape[1]), tok(sqkv.shape[1]),
                  _resident(dww.shape), _resident(dwb.shape)],
        out_specs=[tok(dc), tok(d), tok(d), tok(d), tok(d), tok(d), tok(d), tok(LANES),
                   tok(sconf.shape[1]), tok(sqkv.shape[1])],
        out_shape=[jax.ShapeDtypeStruct((n, dc), F32),
                   jax.ShapeDtypeStruct((n, d), F32), jax.ShapeDtypeStruct((n, d), F32),
                   jax.ShapeDtypeStruct((n, d), F32), jax.ShapeDtypeStruct((n, d), BF16),
                   jax.ShapeDtypeStruct((n, d), BF16), jax.ShapeDtypeStruct((n, d), BF16),
                   jax.ShapeDtypeStruct((n, LANES), F32),
                   jax.ShapeDtypeStruct(sconf.shape, F32), jax.ShapeDtypeStruct(sqkv.shape, F32)],
        compiler_params=_params(1),
        name="pre_sample",
    )(x, mod_s, n1w, w_main, w_tail, cw, alog_p, dtb_p, sconf, sqkv, dww, dwb)


def _half_block_mask(row, col, size):
    shift = size.bit_length() - 1
    half = size // 2
    return ((lax.shift_right_logical(row, shift) == lax.shift_right_logical(col, shift))
            & ((row & half) != 0) & ((col & half) == 0))


def _block_diag2(a, b):
    z = jnp.zeros(a.shape, a.dtype)
    return jnp.concatenate([jnp.concatenate([a, z], axis=1), jnp.concatenate([z, b], axis=1)], axis=0)


def _pair_dot(a, b):
    n = b[0].shape[1]
    out = _dot(jnp.concatenate(a, axis=1), _block_diag2(b[0], b[1]))
    return [out[:, :n], out[:, n:]]


def _gdn_prompt_kernel(q_ref, k_ref, v_ref, bg_ref, bgt_ref, o_ref, sfin_ref, s_scr, *, w, nw, n_steps):
    step = pl.program_id(1)

    @pl.when(step == 0)
    def _():
        s_scr[...] = jnp.zeros(s_scr.shape, F32)

    nc = w // CHUNK
    heads = range(N_HEADS)
    units = [(win, h) for win in range(nw) for h in heads]
    n_units = len(units)
    upairs = range(0, n_units, 2)
    row = lax.broadcasted_iota(jnp.int32, (w, w), 0)
    col = lax.broadcasted_iota(jnp.int32, (w, w), 1)
    shift = CHUNK.bit_length() - 1
    same = lax.shift_right_logical(row, shift) == lax.shift_right_logical(col, shift)
    m_incl = same & (col <= row)
    m_strict = same & (col < row)
    eye = (row == col).astype(F32)

    rs = [slice(win * w, (win + 1) * w) for win, _ in units]
    hs = [slice(h * HEAD_DIM, (h + 1) * HEAD_DIM) for _, h in units]
    qh = [q_ref[rs[u], hs[u]] for u in range(n_units)]
    kh = [k_ref[rs[u], hs[u]] for u in range(n_units)]
    vh = [v_ref[rs[u], hs[u]] for u in range(n_units)]
    bcol, gcb, decay = [], [], []
    for u, (win, h) in enumerate(units):
        bcol.append(jnp.broadcast_to(bg_ref[rs[u], h:h + 1], (w, HEAD_DIM)))
        gcb.append(jnp.broadcast_to(bg_ref[rs[u], N_HEADS + h:N_HEADS + h + 1], (w, HEAD_DIM)))
        grow = bgt_ref[N_HEADS + h:N_HEADS + h + 1, rs[u]]
        decay.append(jnp.exp(jnp.where(m_incl, gcb[u] - grow, -1e30)))

    kb = [kh[u].astype(BF16) for u in range(n_units)]
    qb = [qh[u].astype(BF16) for u in range(n_units)]
    amat, att = [None] * n_units, [None] * n_units
    for p in upairs:
        lhs = jnp.concatenate([jnp.concatenate([kb[p], kb[p + 1]], axis=1),
                               jnp.concatenate([qb[p], qb[p + 1]], axis=1)], axis=0)
        r = _dot_nt(lhs, _block_diag2(kb[p], kb[p + 1]))
        for j in range(2):
            kk = r[:w, j * w:(j + 1) * w]
            amat[p + j] = jnp.where(m_strict, kk * bcol[p + j] * decay[p + j], 0.0)
            att[p + j] = (r[w:, j * w:(j + 1) * w] * decay[p + j]).astype(BF16)

    x = [eye - jnp.where(_half_block_mask(row, col, 2), amat[u], 0.0) for u in range(n_units)]
    size = 4
    while size <= CHUNK:
        mask = _half_block_mask(row, col, size)
        for p in upairs:
            blk = [jnp.where(mask, amat[p + j], 0.0).astype(BF16) for j in range(2)]
            xb = [x[p + j].astype(BF16) for j in range(2)]
            y = _pair_dot(blk, xb)
            upd = _pair_dot(xb, [y[j].astype(BF16) for j in range(2)])
            for j in range(2):
                x[p + j] = x[p + j] - upd[j]
        size *= 2

    eg = [jnp.exp(gcb[u]) for u in range(n_units)]
    kv = []
    for u in range(n_units):
        rhs = jnp.concatenate([kh[u] * (bcol[u] * eg[u]), vh[u] * bcol[u]], axis=1).astype(BF16)
        kv.append(_dot(x[u].astype(BF16), rhs).astype(BF16))
    qe, glast, ktt = [], [], []
    for u in range(n_units):
        qe.append(qh[u] * eg[u])
        gl = jnp.concatenate(
            [jnp.broadcast_to(gcb[u][(c + 1) * CHUNK - 1:(c + 1) * CHUNK, :], (CHUNK, HEAD_DIM)) for c in range(nc)],
            axis=0)
        glast.append(gl)
        ktt.append((kh[u] * jnp.exp(gl - gcb[u])).T.astype(BF16))
    wmat = [[None] * nc for _ in range(n_units)]
    bmat = [[None] * nc for _ in range(n_units)]
    qmat = [[None] * nc for _ in range(n_units)]
    omat = [[None] * nc for _ in range(n_units)]
    zeros = jnp.zeros((CHUNK, 2 * HEAD_DIM), BF16)
    for c in range(nc):
        cs = slice(c * CHUNK, (c + 1) * CHUNK)
        for u in range(n_units):
            parts = [zeros] * nc
            parts[c] = kv[u][cs]
            kv_c = jnp.concatenate(parts, axis=0)
            r = _dot(jnp.concatenate([ktt[u], att[u][cs]], axis=0), kv_c)
            wmat[u][c] = r[:HEAD_DIM, :HEAD_DIM]
            bmat[u][c] = r[:HEAD_DIM, HEAD_DIM:]
            qmat[u][c] = qe[u][cs] - r[HEAD_DIM:, :HEAD_DIM]
            omat[u][c] = r[HEAD_DIM:, HEAD_DIM:]

    s = [s_scr[h] for h in heads]
    for win in range(nw):
        for c in range(nc):
            cs = slice(win * w + c * CHUNK, win * w + (c + 1) * CHUNK)
            for hp in range(0, N_HEADS, 2):
                u = win * N_HEADS + hp
                lhs = [jnp.concatenate([wmat[u + j][c], qmat[u + j][c]], axis=0).astype(BF16) for j in range(2)]
                r = _pair_dot(lhs, [s[hp + j].astype(BF16) for j in range(2)])
                for j in range(2):
                    o_ref[cs, hs[u + j]] = r[j][HEAD_DIM:] + omat[u + j][c]
                    decay_c = jnp.exp(glast[u + j][c * CHUNK:c * CHUNK + 1, :])
                    s[hp + j] = s[hp + j] * decay_c - r[j][:HEAD_DIM] + bmat[u + j][c]
    for h in heads:
        s_scr[h] = s[h]

    @pl.when(step == n_steps - 1)
    def _():
        sfin_ref[0] = s_scr[...]


def _gdn_prompt_call(qn, kn, v, bg, bgt, nb, seq_len):
    t, d = qn.shape
    w = GDN_WINDOW
    nw = GDN_WINDOWS_PER_STEP
    assert seq_len % (w * nw) == 0 and w % CHUNK == 0 and w == HEAD_DIM
    n_steps = seq_len // (w * nw)
    tok = pl.BlockSpec((w * nw, d), lambda b, i: (b * n_steps + i, 0))
    kern = functools.partial(_gdn_prompt_kernel, w=w, nw=nw, n_steps=n_steps)
    return pl.pallas_call(
        kern,
        grid=(nb, n_steps),
        in_specs=[tok, tok, tok,
                  pl.BlockSpec((w * nw, LANES), lambda b, i: (b * n_steps + i, 0)),
                  pl.BlockSpec((LANES, w * nw), lambda b, i: (0, b * n_steps + i))],
        out_specs=[tok, pl.BlockSpec((1, N_HEADS, HEAD_DIM, HEAD_DIM), lambda b, i: (b, 0, 0, 0))],
        out_shape=[jax.ShapeDtypeStruct((t, d), F32),
                   jax.ShapeDtypeStruct((nb, N_HEADS, HEAD_DIM, HEAD_DIM), F32)],
        scratch_shapes=[pltpu.VMEM((N_HEADS, HEAD_DIM, HEAD_DIM), F32)],
        compiler_params=_params(2),
        name="gdn_prompt",
    )(qn, kn, v, bg, bgt)


def _gdn_sample_kernel(q_ref, k_ref, v_ref, bg_ref, s_ref, o_ref, snew_ref, *, bs):
    row = lax.broadcasted_iota(jnp.int32, (HEAD_DIM, HEAD_DIM), 0)
    col = lax.broadcasted_iota(jnp.int32, (HEAD_DIM, HEAD_DIM), 1)
    eye = row == col
    for h in range(N_HEADS):
        hs = slice(h * HEAD_DIM, (h + 1) * HEAD_DIM)
        qh = q_ref[:, hs]
        kh = k_ref[:, hs]
        vh = v_ref[:, hs]
        beta = jnp.broadcast_to(bg_ref[:, h:h + 1], (bs, HEAD_DIM))
        eg = jnp.exp(jnp.broadcast_to(bg_ref[:, N_HEADS + h:N_HEADS + h + 1], (bs, HEAD_DIM)))
        att = jnp.sum(qh * kh, axis=-1, keepdims=True)
        for i in range(bs):
            s = s_ref[i, h]
            kcol = jnp.sum(jnp.where(eye, jnp.broadcast_to(kh[i:i + 1, :], (HEAD_DIM, HEAD_DIM)), 0.0),
                           axis=-1, keepdims=True)
            qcol = jnp.sum(jnp.where(eye, jnp.broadcast_to(qh[i:i + 1, :], (HEAD_DIM, HEAD_DIM)), 0.0),
                           axis=-1, keepdims=True)
            ks = jnp.sum(kcol * s, axis=0, keepdims=True)
            qs = jnp.sum(qcol * s, axis=0, keepdims=True)
            egi = eg[i:i + 1, :]
            v_new = beta[i:i + 1, :] * (vh[i:i + 1, :] - egi * ks)
            o_ref[i:i + 1, hs] = egi * qs + att[i:i + 1, :] * v_new
            snew_ref[i, h] = s * egi + kcol * v_new


def _gdn_sample_call(qn, kn, v, bg, s0):
    n, d = qn.shape
    bs = SAMPLE_SEQ_BLOCK
    assert n % bs == 0
    tok = lambda w: pl.BlockSpec((bs, w), lambda i: (i, 0))
    st = pl.BlockSpec((bs, N_HEADS, HEAD_DIM, HEAD_DIM), lambda i: (i, 0, 0, 0))
    return pl.pallas_call(
        functools.partial(_gdn_sample_kernel, bs=bs),
        grid=(n // bs,),
        in_specs=[tok(d), tok(d), tok(d), tok(LANES), st],
        out_specs=[tok(d), st],
        out_shape=[jax.ShapeDtypeStruct((n, d), F32), jax.ShapeDtypeStruct(s0.shape, F32)],
        compiler_params=_params(1),
        name="gdn_sample",
    )(qn, kn, v, bg, s0)


def _rms(x, w):
    return x * lax.rsqrt(jnp.mean(x * x, axis=-1, keepdims=True) + EPS) * w


def _ln_swish(a, lnw_ref, lnb_ref):
    mu = jnp.mean(a, axis=-1, keepdims=True)
    ac = a - mu
    var = jnp.mean(ac * ac, axis=-1, keepdims=True)
    return _silu(ac * lax.rsqrt(var + EPS) * lnw_ref[...] + lnb_ref[...]).astype(BF16)


def _conv_ln_swish_jobs(ext_scr, sh_scr, cv_scr, a_scr, dww_ref, dwb_ref, lnw_ref, lnb_ref, *, tm, dc):
    n_taps = dww_ref.shape[0]
    rows = tm + CONV_HALO - SUBLANES

    def shifts():
        for r in range(1, SUBLANES):
            sh_scr[r - 1] = ext_scr[r:r + rows, :]

    def taps(g):
        cols = slice(g * LANES, (g + 1) * LANES)
        acc = None
        for j in range(n_taps):
            off = CONV_HALO - (n_taps - 1) + j
            r, base = off % SUBLANES, off - off % SUBLANES
            src = ext_scr[base:base + tm, cols] if r == 0 else sh_scr[r - 1, base:base + tm, cols]
            term = dww_ref[j:j + 1, cols] * src
            acc = term if acc is None else acc + term
        cv_scr[:, cols] = acc + dwb_ref[:, cols]

    def finish():
        a_scr[...] = _ln_swish(cv_scr[...], lnw_ref, lnb_ref)

    return [shifts] + [functools.partial(taps, g) for g in range(dc // LANES)] + [finish]


def _post_tail(a, x_ref, mod, o_ref, z_ref, gnw_ref, ga_ref, gb_ref, wca_ref, wgo_ref, wo_ref, n2w_ref,
               w1_ref, w2_ref, fnw_ref, y_ref, *, dff, side_jobs=()):
    side = list(side_jobs)

    def run_side():
        if side:
            side.pop(0)()

    y_a = _dot(a, wca_ref[...])
    run_side()

    gated = []
    for h in range(N_HEADS):
        hs = slice(h * HEAD_DIM, (h + 1) * HEAD_DIM)
        gated.append(_rms(o_ref[:, hs], gnw_ref[...]) * _silu(z_ref[:, hs].astype(F32)))
    y_b = _dot(jnp.concatenate(gated, axis=1).astype(BF16), wgo_ref[...])
    run_side()

    merged = ga_ref[...].astype(F32) * y_a + gb_ref[...].astype(F32) * y_b
    x1 = x_ref[...] + mod(2) * _dot(merged.astype(BF16), wo_ref[...])
    run_side()

    h2 = (_rms(x1, n2w_ref[...]) * (1.0 + mod(4)) + mod(3)).astype(BF16)
    step = 1024
    ff = None
    for c in range(dff // step):
        f = jnp.maximum(_dot(h2, w1_ref[:, c * step:(c + 1) * step]), 0.0)
        part = _dot((f * f).astype(BF16), w2_ref[c * step:(c + 1) * step, :])
        ff = part if ff is None else ff + part
        run_side()
    while side:
        run_side()
    x2 = x1 + mod(5) * ff
    y_ref[...] = _rms(x2, fnw_ref[...])


def _post_prompt_kernel(x_ref, mod_ref, glu_ref, glun_ref, dww_ref, dwb_ref, lnw_ref, lnb_ref,
                        o_ref, z_ref, gnw_ref, ga_ref, gb_ref, wca_ref, wgo_ref, wo_ref, n2w_ref,
                        w1_ref, w2_ref, fnw_ref, y_ref, ext_scr, sh_scr, cv_scr, a_scr,
                        *, tm, tiles_per_seq, d, dc, dff):
    i = pl.program_id(0)
    conv_jobs = functools.partial(_conv_ln_swish_jobs, ext_scr, sh_scr, cv_scr, a_scr, dww_ref, dwb_ref,
                                  lnw_ref, lnb_ref, tm=tm, dc=dc)

    @pl.when(i == 0)
    def _():
        ext_scr[0:CONV_HALO, :] = jnp.zeros((CONV_HALO, dc), F32)
        ext_scr[CONV_HALO:CONV_HALO + tm, :] = glu_ref[...]
        for job in conv_jobs():
            job()

    a = a_scr[...]
    ext_scr[0:CONV_HALO, :] = jnp.where((i + 1) % tiles_per_seq == 0, 0.0, glu_ref[tm - CONV_HALO:tm, :])
    ext_scr[CONV_HALO:CONV_HALO + tm, :] = glun_ref[...]

    mod = lambda k: mod_ref[0, :, k * d:(k + 1) * d]
    _post_tail(a, x_ref, mod, o_ref, z_ref, gnw_ref, ga_ref, gb_ref, wca_ref, wgo_ref, wo_ref, n2w_ref,
               w1_ref, w2_ref, fnw_ref, y_ref, dff=dff, side_jobs=conv_jobs())


def _post_sample_kernel(x_ref, mod_ref, apre_ref, lnw_ref, lnb_ref,
                        o_ref, z_ref, gnw_ref, ga_ref, gb_ref, wca_ref, wgo_ref, wo_ref, n2w_ref,
                        w1_ref, w2_ref, fnw_ref, y_ref, *, d, dff):
    a = _ln_swish(apre_ref[...], lnw_ref, lnb_ref)
    mod = lambda k: mod_ref[:, k * d:(k + 1) * d]
    _post_tail(a, x_ref, mod, o_ref, z_ref, gnw_ref, ga_ref, gb_ref, wca_ref, wgo_ref, wo_ref, n2w_ref,
               w1_ref, w2_ref, fnw_ref, y_ref, dff=dff)


def _post_call(x, mod, conv_in, lnw, lnb, o, z, gnw, ga, gb, wca, wgo, wo, n2w, w1, w2, fnw, *, seq_len):
    t, d = x.shape
    dc = lnw.shape[1]
    dff = w1.shape[1]
    prompt = len(conv_in) == 3
    tm = min(POST_TILE, t)
    assert t % tm == 0
    n_tiles = t // tm
    tok = lambda w: pl.BlockSpec((tm, w), lambda i: (i, 0))
    common_args = [lnw, lnb, o, z, gnw, ga, gb, wca, wgo, wo, n2w, w1, w2, fnw]
    common_specs = [_resident(lnw.shape), _resident(lnb.shape), tok(d), tok(d), _resident(gnw.shape), tok(d), tok(d),
                    _resident(wca.shape), _resident(wgo.shape), _resident(wo.shape), _resident(n2w.shape),
                    _resident(w1.shape), _resident(w2.shape), _resident(fnw.shape)]
    if prompt:
        glu, dww, dwb = conv_in
        assert seq_len % tm == 0 and tm >= CONV_HALO >= dww.shape[0] - 1
        tiles_per_seq = seq_len // tm
        args = [x, mod, glu, glu, dww, dwb] + common_args
        specs = [tok(d), pl.BlockSpec((1, 1, mod.shape[2]), lambda i: (i // tiles_per_seq, 0, 0)),
                 tok(dc), pl.BlockSpec((tm, dc), lambda i: (jnp.minimum(i + 1, n_tiles - 1), 0)),
                 _resident(dww.shape), _resident(dwb.shape)] + common_specs
        scratch = [pltpu.VMEM((tm + CONV_HALO, dc), F32),
                   pltpu.VMEM((SUBLANES - 1, tm + CONV_HALO - SUBLANES, dc), F32),
                   pltpu.VMEM((tm, dc), F32),
                   pltpu.VMEM((tm, dc), BF16)]
        kern = functools.partial(_post_prompt_kernel, tm=tm, tiles_per_seq=tiles_per_seq, d=d, dc=dc, dff=dff)
        name = "post_prompt"
    else:
        args = [x, mod, conv_in[0]] + common_args
        specs = [tok(d), tok(mod.shape[1]), tok(dc)] + common_specs
        scratch = []
        kern = functools.partial(_post_sample_kernel, d=d, dff=dff)
        name = "post_sample"
    return pl.pallas_call(
        kern,
        grid=(n_tiles,),
        in_specs=specs,
        out_specs=tok(d),
        out_shape=jax.ShapeDtypeStruct((t, d), F32),
        scratch_shapes=scratch,
        compiler_params=_params(1),
        name=name,
    )(*args)


def _split_w_in(w_in, d, dc, dqkv):
    o1 = 2 * dc + dqkv + d
    ba = w_in[:, o1:o1 + 2 * N_HEADS]
    gates = w_in[:, o1 + 2 * N_HEADS:]
    pad = jnp.zeros((w_in.shape[0], LANES - 2 * N_HEADS), w_in.dtype)
    return w_in[:, :o1].astype(BF16), jnp.concatenate([gates, ba, pad], axis=1).astype(BF16)


def _lane_pad(vec, offset):
    out = jnp.zeros((1, LANES), F32)
    return out.at[0, offset:offset + vec.shape[0]].set(vec.astype(F32))


def kernel(x_prompt, x_sample, c_prompt, c_sample, state_conf_conv, state_qkv_conv, state_delta, w_ada, b_ada, norm1_w, w_in, conf_dw_w, conf_dw_b, conf_ln_w, conf_ln_b, w_conf_out, gdn_conv_w, a_log, dt_bias, gdn_norm_w, w_gdn_out, w_o, norm2_w, w_ff1, w_ff2, final_norm_w):
    nb, seq_len, d = x_prompt.shape
    ns = x_sample.shape[0]
    depth = w_ada.shape[0]
    dc = conf_dw_w.shape[2]
    dqkv = gdn_conv_w.shape[2]
    n_hist = conf_dw_w.shape[1] - 1
    n_taps = gdn_conv_w.shape[1]
    assert x_sample.shape[1] == 1

    xp = x_prompt.reshape(nb * seq_len, d)
    xs = x_sample.reshape(ns, d)
    c_all = jnp.concatenate([c_sample, c_prompt], axis=0)
    row2 = lambda v: v.reshape(1, -1).astype(F32)

    conf_p, qkv_p, delta_p, conf_s, qkv_s, delta_s = [], [], [], [], [], []
    for l in range(depth):
        mod = _mod_call(c_all, w_ada[l].astype(F32), row2(b_ada[l]))
        mod_p = mod[ns:].reshape(nb, 1, N_MOD * d)
        mod_s = mod
        w_main, w_tail = _split_w_in(w_in[l], d, dc, dqkv)
        n1w, n2w, fnw = row2(norm1_w[l]), row2(norm2_w[l]), row2(final_norm_w)
        alog_p = _lane_pad(a_log[l], N_HEADS)
        dtb_p = _lane_pad(dt_bias[l], N_HEADS)
        cw = gdn_conv_w[l].astype(F32)
        dww, dwb = conf_dw_w[l].astype(F32), row2(conf_dw_b[l])
        lnw, lnb = row2(conf_ln_w[l]), row2(conf_ln_b[l])
        gnw = row2(gdn_norm_w[l])
        wca, wgo, wo = w_conf_out[l].astype(BF16), w_gdn_out[l].astype(BF16), w_o[l].astype(BF16)
        w1, w2 = w_ff1[l].astype(BF16), w_ff2[l].astype(BF16)
        assert depth == 1

        glu, qn, kn, v, z, ga, gb, bg, bgt, tail = _pre_prompt_call(
            xp, mod_p, n1w, w_main, w_tail, cw, alog_p, dtb_p, seq_len)
        o, s_fin = _gdn_prompt_call(qn, kn, v, bg, bgt, nb, seq_len)
        xp = _post_call(xp, mod_p, (glu, dww, dwb), lnw, lnb, o, z, gnw, ga, gb, wca, wgo, wo, n2w, w1, w2, fnw,
                        seq_len=seq_len)
        conf_p.append(glu.reshape(nb, seq_len, dc)[:, seq_len - n_hist:])
        qkv_p.append(tail.reshape(nb, SUBLANES, dqkv)[:, SUBLANES - (n_taps - 1):])
        delta_p.append(s_fin)

        sconf = state_conf_conv[l].reshape(ns, n_hist * dc)
        sqkv = state_qkv_conv[l].reshape(ns, (n_taps - 1) * dqkv)
        apre, qn, kn, v, z, ga, gb, bg, nconf, nqkv = _pre_sample_call(
            xs, mod_s, n1w, w_main, w_tail, cw, alog_p, dtb_p, sconf, sqkv, dww, dwb)
        o, s_new = _gdn_sample_call(qn, kn, v, bg, state_delta[l].astype(F32))
        xs = _post_call(xs, mod_s, (apre,), lnw, lnb, o, z, gnw, ga, gb, wca, wgo, wo, n2w, w1, w2, fnw,
                        seq_len=1)
        conf_s.append(nconf.reshape(ns, n_hist, dc))
        qkv_s.append(nqkv.reshape(ns, n_taps - 1, dqkv))
        delta_s.append(s_new)

    return (xp.reshape(nb, seq_len, d), xs.reshape(ns, 1, d),
            jnp.stack(conf_p), jnp.stack(qkv_p), jnp.stack(delta_p),
            jnp.stack(conf_s), jnp.stack(qkv_s), jnp.stack(delta_s))
```

```python
import functools

import jax
import jax.numpy as jnp
from jax import lax
from jax.experimental import pallas as pl
from jax.experimental.pallas import tpu as pltpu

F32 = jnp.float32
BF16 = jnp.bfloat16

EPS = 1e-6
N_MOD = 6
N_HEADS = 8
HEAD_DIM = 128
CHUNK = 64
LANES = 128
SUBLANES = 8
VMEM_LIMIT_BYTES = 56 * 1024 * 1024

PRE_TILE = 256
POST_TILE = 256
GDN_WINDOW = 128
GDN_WINDOWS_PER_STEP = 4
CONV_HALO = 32
SAMPLE_SEQ_BLOCK = 16


NEG_LOG2_E = -1.4426950408889634


def _sigmoid(x):
    return 1.0 / (1.0 + jnp.exp2(x * NEG_LOG2_E))


def _silu(x):
    return x * _sigmoid(x)


def _softplus(x):
    return jnp.maximum(x, 0.0) + jnp.log(1.0 + jnp.exp(-jnp.abs(x)))


def _dot(a, b):
    return jnp.dot(a, b, preferred_element_type=F32)


def _dot_nt(a, b):
    return lax.dot_general(a, b, (((1,), (1,)), ((), ())), preferred_element_type=F32)


def _resident(shape):
    n = len(shape)
    return pl.BlockSpec(shape, lambda *_: (0,) * n, pipeline_mode=pl.Buffered(1))


def _params(n_grid):
    return pltpu.CompilerParams(dimension_semantics=("arbitrary",) * n_grid,
                                vmem_limit_bytes=VMEM_LIMIT_BYTES)


def _mod_kernel(c_ref, w_ref, b_ref, o_ref):
    c = c_ref[...]
    o_ref[...] = _dot(_silu(c).astype(BF16), w_ref[...].astype(BF16)) + b_ref[...]


def _mod_call(c_all, w_ada, b_ada):
    n, d = c_all.shape
    e = w_ada.shape[1]
    tn = e // 8
    return pl.pallas_call(
        _mod_kernel,
        grid=(e // tn,),
        in_specs=[pl.BlockSpec((n, d), lambda j: (0, 0)),
                  pl.BlockSpec((d, tn), lambda j: (0, j)),
                  pl.BlockSpec((1, tn), lambda j: (0, j))],
        out_specs=pl.BlockSpec((n, tn), lambda j: (0, j)),
        out_shape=jax.ShapeDtypeStruct((n, e), F32),
        compiler_params=_params(1),
        name="mod",
    )(c_all, w_ada, b_ada)


def _pre_front(x, shift1, scale1, n1w):
    ms = jnp.mean(x * x, axis=-1, keepdims=True)
    h = x * lax.rsqrt(ms + EPS) * n1w
    return (h * (1.0 + scale1) + shift1).astype(BF16)


def _qkv_finish(y, g, qkv_ref):
    s = _silu(y)
    n_qk = qkv_ref.shape[1] // (3 * LANES)
    if g < 2 * n_qk:
        s = s * lax.rsqrt(jnp.sum(s * s, axis=-1, keepdims=True) + EPS)
    if g < n_qk:
        s = s * (HEAD_DIM ** -0.5)
    qkv_ref[:, g * LANES:(g + 1) * LANES] = s


def _beta_decay(ba, alog, dtb):
    lane = lax.broadcasted_iota(jnp.int32, ba.shape, 1)
    beta = _sigmoid(ba)
    g = -jnp.exp(alog) * _softplus(ba + dtb)
    return jnp.where(lane < N_HEADS, beta, g)


def _pre_prompt_kernel(x_ref, mod_ref, n1w_ref, w_ref, wt_ref, cw_ref, alog_ref, dtb_ref, cum_ref,
                       glu_ref, qkv_ref, zg_ref, bg_ref, bgt_ref, tail_ref,
                       qkv_scr, h_scr, *, tm, tiles_per_seq, d, dc, dqkv):
    i = pl.program_id(0)
    h_scr[...] = _pre_front(x_ref[...], mod_ref[0, :, 0:d], mod_ref[0, :, d:2 * d], n1w_ref[...])

    u = _dot(h_scr[...], w_ref[:, 0:2 * dc])
    glu_ref[...] = u[:, :dc] * _sigmoid(u[:, dc:])

    @pl.when(i % tiles_per_seq == 0)
    def _():
        qkv_scr[0:SUBLANES, :] = jnp.zeros((SUBLANES, dqkv), F32)

    o_qkv = 2 * dc
    o_z = o_qkv + dqkv
    step = 512
    n_chunks = dqkv // step
    n_taps = cw_ref.shape[0]

    def project(c):
        cols = slice(c * step, (c + 1) * step)
        qkv_scr[SUBLANES:SUBLANES + tm, cols] = _dot(h_scr[...], w_ref[:, o_qkv + c * step:o_qkv + (c + 1) * step])
        tail_ref[:, cols] = qkv_scr[tm:tm + SUBLANES, cols]

    def conv(c):
        for g in range(c * step // LANES, (c + 1) * step // LANES):
            cols = slice(g * LANES, (g + 1) * LANES)
            y = None
            for j in range(n_taps):
                r0 = SUBLANES - (n_taps - 1) + j
                term = cw_ref[j:j + 1, cols] * qkv_scr[r0:r0 + tm, cols]
                y = term if y is None else y + term
            _qkv_finish(y, g, qkv_ref)
        cols = slice(c * step, (c + 1) * step)
        qkv_scr[0:SUBLANES, cols] = qkv_scr[tm:tm + SUBLANES, cols]

    pieces = [(wsrc, o_w + half * step, k * d + half * step, act)
              for k, (wsrc, o_w, act) in enumerate([(w_ref, o_z, False), (wt_ref, 0, True), (wt_ref, d, True)])
              for half in range(d // step)]
    assert len(pieces) == n_chunks

    def side(c):
        wsrc, o_w, o_out, act = pieces[c]
        r = _dot(h_scr[...], wsrc[:, o_w:o_w + step])
        zg_ref[:, o_out:o_out + step] = (_sigmoid(r) if act else r).astype(BF16)

    project(0)
    for c in range(n_chunks):
        if c + 1 < n_chunks:
            project(c + 1)
        side(c)
        conv(c)

    ba = _dot(h_scr[...], wt_ref[:, 2 * d:2 * d + LANES])
    bg = _beta_decay(ba, alog_ref[...], dtb_ref[...])
    bg_hi = bg.astype(BF16)
    bg_lo = (bg - bg_hi.astype(F32)).astype(BF16)
    cum = _dot(cum_ref[...], bg_hi) + _dot(cum_ref[...], bg_lo)
    lane = lax.broadcasted_iota(jnp.int32, bg.shape, 1)
    bg = jnp.where(lane < N_HEADS, bg, cum)
    bg_ref[...] = bg
    bgt_ref[...] = bg.T


def _pre_prompt_call(x, mod_p, n1w, w_bf, w_tail, cw, alog_p, dtb_p, seq_len, dc):
    t, d = x.shape
    tm = PRE_TILE
    assert seq_len % tm == 0 and tm % CHUNK == 0
    dqkv = cw.shape[1]
    w_main_shape = (d, 2 * dc + dqkv + d)
    nb = t // seq_len
    tiles_per_seq = seq_len // tm
    row = lax.broadcasted_iota(jnp.int32, (tm, tm), 0)
    col = lax.broadcasted_iota(jnp.int32, (tm, tm), 1)
    cum_mat = ((row // CHUNK == col // CHUNK) & (col <= row)).astype(BF16)

    tok = lambda w: pl.BlockSpec((tm, w), lambda i: (i, 0))
    kern = functools.partial(_pre_prompt_kernel, tm=tm, tiles_per_seq=tiles_per_seq, d=d, dc=dc, dqkv=dqkv)
    return pl.pallas_call(
        kern,
        grid=(t // tm,),
        in_specs=[tok(d),
                  pl.BlockSpec((1, 1, mod_p.shape[2]), lambda i: (i // tiles_per_seq, 0, 0)),
                  _resident(n1w.shape), _resident(w_main_shape), _resident(w_tail.shape), _resident(cw.shape),
                  _resident(alog_p.shape), _resident(dtb_p.shape), _resident(cum_mat.shape)],
        out_specs=[tok(dc), tok(dqkv), tok(3 * d), tok(LANES),
                   pl.BlockSpec((LANES, tm), lambda i: (0, i)),
                   pl.BlockSpec((SUBLANES, dqkv), lambda i: (i // tiles_per_seq, 0))],
        out_shape=[jax.ShapeDtypeStruct((t, dc), F32),
                   jax.ShapeDtypeStruct((t, dqkv), F32), jax.ShapeDtypeStruct((t, 3 * d), BF16),
                   jax.ShapeDtypeStruct((t, LANES), F32),
                   jax.ShapeDtypeStruct((LANES, t), F32),
                   jax.ShapeDtypeStruct((nb * SUBLANES, dqkv), F32)],
        scratch_shapes=[pltpu.VMEM((tm + SUBLANES, dqkv), F32), pltpu.VMEM((tm, d), BF16)],
        compiler_params=_params(1),
        name="pre_prompt",
    )(x, mod_p, n1w, w_bf, w_tail, cw, alog_p, dtb_p, cum_mat)


def _pre_sample_kernel(x_ref, mod_ref, n1w_ref, w_ref, wt_ref, cw_ref, alog_ref, dtb_ref,
                       sconf_ref, sqkv_ref, dww_ref, dwb_ref,
                       apre_ref, qkv_ref, zg_ref, bg_ref,
                       nconf_ref, nqkv_ref, *, d, dc, dqkv):
    hb = _pre_front(x_ref[...], mod_ref[:, 0:d], mod_ref[:, d:2 * d], n1w_ref[...])

    u = _dot(hb, w_ref[:, 0:2 * dc])
    glu = u[:, :dc] * _sigmoid(u[:, dc:])
    n_hist = sconf_ref.shape[1] // dc
    acc = dww_ref[n_hist:n_hist + 1, :] * glu + dwb_ref[...]
    for j in range(n_hist):
        acc = acc + dww_ref[j:j + 1, :] * sconf_ref[:, j * dc:(j + 1) * dc]
    apre_ref[...] = acc
    nconf_ref[:, 0:(n_hist - 1) * dc] = sconf_ref[:, dc:n_hist * dc]
    nconf_ref[:, (n_hist - 1) * dc:n_hist * dc] = glu

    o_qkv = 2 * dc
    n_taps = cw_ref.shape[0]
    nqkv_ref[:, 0:(n_taps - 2) * dqkv] = sqkv_ref[:, dqkv:(n_taps - 1) * dqkv]
    for g in range(dqkv // LANES):
        cols = slice(g * LANES, (g + 1) * LANES)
        raw = _dot(hb, w_ref[:, o_qkv + g * LANES:o_qkv + (g + 1) * LANES])
        nqkv_ref[:, (n_taps - 2) * dqkv + g * LANES:(n_taps - 2) * dqkv + (g + 1) * LANES] = raw
        y = cw_ref[n_taps - 1:n_taps, cols] * raw
        for j in range(n_taps - 1):
            y = y + cw_ref[j:j + 1, cols] * sqkv_ref[:, j * dqkv + g * LANES:j * dqkv + (g + 1) * LANES]
        _qkv_finish(y, g, qkv_ref)

    o_z = o_qkv + dqkv
    zg_ref[:, 0:d] = _dot(hb, w_ref[:, o_z:o_z + d]).astype(BF16)
    zg_ref[:, d:2 * d] = _sigmoid(_dot(hb, wt_ref[:, 0:d])).astype(BF16)
    zg_ref[:, 2 * d:3 * d] = _sigmoid(_dot(hb, wt_ref[:, d:2 * d])).astype(BF16)
    ba = _dot(hb, wt_ref[:, 2 * d:2 * d + LANES])
    bg_ref[...] = _beta_decay(ba, alog_ref[...], dtb_ref[...])


def _pre_sample_call(x, mod_s, n1w, w_bf, w_tail, cw, alog_p, dtb_p, sconf, sqkv, dww, dwb):
    n, d = x.shape
    dqkv = cw.shape[1]
    dc = dwb.shape[1]
    w_main_shape = (d, 2 * dc + dqkv + d)
    tm = 32
    assert n % tm == 0
    tok = lambda w: pl.BlockSpec((tm, w), lambda i: (i, 0))
    kern = functools.partial(_pre_sample_kernel, d=d, dc=dc, dqkv=dqkv)
    return pl.pallas_call(
        kern,
        grid=(n // tm,),
        in_specs=[tok(d), tok(mod_s.shape[1]),
                  _resident(n1w.shape), _resident(w_main_shape), _resident(w_tail.shape), _resident(cw.shape),
                  _resident(alog_p.shape), _resident(dtb_p.shape),
                  tok(sconf.shape[1]), tok(sqkv.shape[1]),
                  _resident(dww.shape), _resident(dwb.shape)],
        out_specs=[tok(dc), tok(dqkv), tok(3 * d), tok(LANES),
                   tok(sconf.shape[1]), tok(sqkv.shape[1])],
        out_shape=[jax.ShapeDtypeStruct((n, dc), F32),
                   jax.ShapeDtypeStruct((n, dqkv), F32), jax.ShapeDtypeStruct((n, 3 * d), BF16),
                   jax.ShapeDtypeStruct((n, LANES), F32),
                   jax.ShapeDtypeStruct(sconf.shape, F32), jax.ShapeDtypeStruct(sqkv.shape, F32)],
        compiler_params=_params(1),
        name="pre_sample",
    )(x, mod_s, n1w, w_bf, w_tail, cw, alog_p, dtb_p, sconf, sqkv, dww, dwb)


def _half_block_mask(row, col, size):
    shift = size.bit_length() - 1
    half = size // 2
    return ((lax.shift_right_logical(row, shift) == lax.shift_right_logical(col, shift))
            & ((row & half) != 0) & ((col & half) == 0))


def _block_diag2(a, b):
    z = jnp.zeros(a.shape, a.dtype)
    return jnp.concatenate([jnp.concatenate([a, z], axis=1), jnp.concatenate([z, b], axis=1)], axis=0)


def _pair_dot(a, b):
    n = b[0].shape[1]
    out = _dot(jnp.concatenate(a, axis=1), _block_diag2(b[0], b[1]))
    return [out[:, :n], out[:, n:]]


def _gdn_prompt_kernel(q_ref, k_ref, v_ref, bg_ref, bgt_ref, o_ref, sfin_ref, s_scr, *, w, nw, n_steps):
    step = pl.program_id(1)

    @pl.when(step == 0)
    def _():
        s_scr[...] = jnp.zeros(s_scr.shape, F32)

    nc = w // CHUNK
    heads = range(N_HEADS)
    units = [(win, h) for win in range(nw) for h in heads]
    n_units = len(units)
    upairs = range(0, n_units, 2)
    row = lax.broadcasted_iota(jnp.int32, (w, w), 0)
    col = lax.broadcasted_iota(jnp.int32, (w, w), 1)
    shift = CHUNK.bit_length() - 1
    same = lax.shift_right_logical(row, shift) == lax.shift_right_logical(col, shift)
    m_incl = same & (col <= row)
    m_strict = same & (col < row)
    eye = (row == col).astype(F32)

    rs = [slice(win * w, (win + 1) * w) for win, _ in units]
    hs = [slice(h * HEAD_DIM, (h + 1) * HEAD_DIM) for _, h in units]
    qh = [q_ref[rs[u], hs[u]] for u in range(n_units)]
    kh = [k_ref[rs[u], hs[u]] for u in range(n_units)]
    vh = [v_ref[rs[u], hs[u]] for u in range(n_units)]
    bcol, gcb, decay = [], [], []
    for u, (win, h) in enumerate(units):
        bcol.append(jnp.broadcast_to(bg_ref[rs[u], h:h + 1], (w, HEAD_DIM)))
        gcb.append(jnp.broadcast_to(bg_ref[rs[u], N_HEADS + h:N_HEADS + h + 1], (w, HEAD_DIM)))
        grow = bgt_ref[N_HEADS + h:N_HEADS + h + 1, rs[u]]
        decay.append(jnp.exp(jnp.where(m_incl, gcb[u] - grow, -1e30)))

    kb = [kh[u].astype(BF16) for u in range(n_units)]
    qb = [qh[u].astype(BF16) for u in range(n_units)]
    amat, att = [None] * n_units, [None] * n_units
    for p in upairs:
        lhs = jnp.concatenate([jnp.concatenate([kb[p], kb[p + 1]], axis=1),
                               jnp.concatenate([qb[p], qb[p + 1]], axis=1)], axis=0)
        r = _dot_nt(lhs, _block_diag2(kb[p], kb[p + 1]))
        for j in range(2):
            kk = r[:w, j * w:(j + 1) * w]
            amat[p + j] = jnp.where(m_strict, kk * bcol[p + j] * decay[p + j], 0.0)
            att[p + j] = (r[w:, j * w:(j + 1) * w] * decay[p + j]).astype(BF16)

    x = [eye - jnp.where(_half_block_mask(row, col, 2), amat[u], 0.0) for u in range(n_units)]
    size = 4
    while size <= CHUNK:
        mask = _half_block_mask(row, col, size)
        for p in upairs:
            blk = [jnp.where(mask, amat[p + j], 0.0).astype(BF16) for j in range(2)]
            xb = [x[p + j].astype(BF16) for j in range(2)]
            y = _pair_dot(blk, xb)
            upd = _pair_dot(xb, [y[j].astype(BF16) for j in range(2)])
            for j in range(2):
                x[p + j] = x[p + j] - upd[j]
        size *= 2

    eg = [jnp.exp(gcb[u]) for u in range(n_units)]
    kv = []
    for u in range(n_units):
        rhs = jnp.concatenate([kh[u] * (bcol[u] * eg[u]), vh[u] * bcol[u]], axis=1).astype(BF16)
        kv.append(_dot(x[u].astype(BF16), rhs).astype(BF16))
    qe, glast, ktt = [], [], []
    for u in range(n_units):
        qe.append(qh[u] * eg[u])
        gl = jnp.concatenate(
            [jnp.broadcast_to(gcb[u][(c + 1) * CHUNK - 1:(c + 1) * CHUNK, :], (CHUNK, HEAD_DIM)) for c in range(nc)],
            axis=0)
        glast.append(gl)
        ktt.append((kh[u] * jnp.exp(gl - gcb[u])).T.astype(BF16))
    wmat = [[None] * nc for _ in range(n_units)]
    bmat = [[None] * nc for _ in range(n_units)]
    qmat = [[None] * nc for _ in range(n_units)]
    omat = [[None] * nc for _ in range(n_units)]
    zeros = jnp.zeros((CHUNK, 2 * HEAD_DIM), BF16)
    for c in range(nc):
        cs = slice(c * CHUNK, (c + 1) * CHUNK)
        for u in range(n_units):
            parts = [zeros] * nc
            parts[c] = kv[u][cs]
            kv_c = jnp.concatenate(parts, axis=0)
            r = _dot(jnp.concatenate([ktt[u], att[u][cs]], axis=0), kv_c)
            wmat[u][c] = r[:HEAD_DIM, :HEAD_DIM]
            bmat[u][c] = r[:HEAD_DIM, HEAD_DIM:]
            qmat[u][c] = qe[u][cs] - r[HEAD_DIM:, :HEAD_DIM]
            omat[u][c] = r[HEAD_DIM:, HEAD_DIM:]

    s = [s_scr[h] for h in heads]
    for win in range(nw):
        for c in range(nc):
            cs = slice(win * w + c * CHUNK, win * w + (c + 1) * CHUNK)
            for hp in range(0, N_HEADS, 2):
                u = win * N_HEADS + hp
                lhs = [jnp.concatenate([wmat[u + j][c], qmat[u + j][c]], axis=0).astype(BF16) for j in range(2)]
                r = _pair_dot(lhs, [s[hp + j].astype(BF16) for j in range(2)])
                for j in range(2):
                    o_ref[cs, hs[u + j]] = r[j][HEAD_DIM:] + omat[u + j][c]
                    decay_c = jnp.exp(glast[u + j][c * CHUNK:c * CHUNK + 1, :])
                    s[hp + j] = s[hp + j] * decay_c - r[j][:HEAD_DIM] + bmat[u + j][c]
    for h in heads:
        s_scr[h] = s[h]

    @pl.when(step == n_steps - 1)
    def _():
        sfin_ref[0] = s_scr[...]


def _gdn_prompt_call(qkv, bg, bgt, nb, seq_len):
    t, d = qkv.shape[0], qkv.shape[1] // 3
    w = GDN_WINDOW
    nw = GDN_WINDOWS_PER_STEP
    assert seq_len % (w * nw) == 0 and w % CHUNK == 0 and w == HEAD_DIM
    n_steps = seq_len // (w * nw)
    tok = pl.BlockSpec((w * nw, d), lambda b, i: (b * n_steps + i, 0))
    part = lambda k: pl.BlockSpec((w * nw, d), lambda b, i: (b * n_steps + i, k))
    kern = functools.partial(_gdn_prompt_kernel, w=w, nw=nw, n_steps=n_steps)
    return pl.pallas_call(
        kern,
        grid=(nb, n_steps),
        in_specs=[part(0), part(1), part(2),
                  pl.BlockSpec((w * nw, LANES), lambda b, i: (b * n_steps + i, 0)),
                  pl.BlockSpec((LANES, w * nw), lambda b, i: (0, b * n_steps + i))],
        out_specs=[tok, pl.BlockSpec((1, N_HEADS, HEAD_DIM, HEAD_DIM), lambda b, i: (b, 0, 0, 0))],
        out_shape=[jax.ShapeDtypeStruct((t, d), F32),
                   jax.ShapeDtypeStruct((nb, N_HEADS, HEAD_DIM, HEAD_DIM), F32)],
        scratch_shapes=[pltpu.VMEM((N_HEADS, HEAD_DIM, HEAD_DIM), F32)],
        compiler_params=_params(2),
        name="gdn_prompt",
    )(qkv, qkv, qkv, bg, bgt)


def _gdn_sample_kernel(q_ref, k_ref, v_ref, bg_ref, s_ref, o_ref, snew_ref, *, bs):
    row = lax.broadcasted_iota(jnp.int32, (HEAD_DIM, HEAD_DIM), 0)
    col = lax.broadcasted_iota(jnp.int32, (HEAD_DIM, HEAD_DIM), 1)
    eye = row == col
    for h in range(N_HEADS):
        hs = slice(h * HEAD_DIM, (h + 1) * HEAD_DIM)
        qh = q_ref[:, hs]
        kh = k_ref[:, hs]
        vh = v_ref[:, hs]
        beta = jnp.broadcast_to(bg_ref[:, h:h + 1], (bs, HEAD_DIM))
        eg = jnp.exp(jnp.broadcast_to(bg_ref[:, N_HEADS + h:N_HEADS + h + 1], (bs, HEAD_DIM)))
        att = jnp.sum(qh * kh, axis=-1, keepdims=True)
        for i in range(bs):
            s = s_ref[i, h]
            kcol = jnp.sum(jnp.where(eye, jnp.broadcast_to(kh[i:i + 1, :], (HEAD_DIM, HEAD_DIM)), 0.0),
                           axis=-1, keepdims=True)
            qcol = jnp.sum(jnp.where(eye, jnp.broadcast_to(qh[i:i + 1, :], (HEAD_DIM, HEAD_DIM)), 0.0),
                           axis=-1, keepdims=True)
            ks = jnp.sum(kcol * s, axis=0, keepdims=True)
            qs = jnp.sum(qcol * s, axis=0, keepdims=True)
            egi = eg[i:i + 1, :]
            v_new = beta[i:i + 1, :] * (vh[i:i + 1, :] - egi * ks)
            o_ref[i:i + 1, hs] = egi * qs + att[i:i + 1, :] * v_new
            snew_ref[i, h] = s * egi + kcol * v_new


def _gdn_sample_call(qkv, bg, s0):
    n, d = qkv.shape[0], qkv.shape[1] // 3
    bs = SAMPLE_SEQ_BLOCK
    assert n % bs == 0
    tok = lambda w: pl.BlockSpec((bs, w), lambda i: (i, 0))
    st = pl.BlockSpec((bs, N_HEADS, HEAD_DIM, HEAD_DIM), lambda i: (i, 0, 0, 0))
    return pl.pallas_call(
        functools.partial(_gdn_sample_kernel, bs=bs),
        grid=(n // bs,),
        in_specs=[pl.BlockSpec((bs, d), lambda i: (i, 0)), pl.BlockSpec((bs, d), lambda i: (i, 1)),
                  pl.BlockSpec((bs, d), lambda i: (i, 2)), tok(LANES), st],
        out_specs=[tok(d), st],
        out_shape=[jax.ShapeDtypeStruct((n, d), F32), jax.ShapeDtypeStruct(s0.shape, F32)],
        compiler_params=_params(1),
        name="gdn_sample",
    )(qkv, qkv, qkv, bg, s0)


def _rms(x, w):
    return x * lax.rsqrt(jnp.mean(x * x, axis=-1, keepdims=True) + EPS) * w


def _ln_swish(a, lnw_ref, lnb_ref):
    mu = jnp.mean(a, axis=-1, keepdims=True)
    ac = a - mu
    var = jnp.mean(ac * ac, axis=-1, keepdims=True)
    return _silu(ac * lax.rsqrt(var + EPS) * lnw_ref[...] + lnb_ref[...]).astype(BF16)


def _conv_ln_swish_jobs(ext_scr, sh_scr, cv_scr, a_scr, dww_ref, dwb_ref, lnw_ref, lnb_ref, *, tm, dc):
    n_taps = dww_ref.shape[0]
    rows = tm + CONV_HALO - SUBLANES

    def shifts():
        for r in range(1, SUBLANES):
            sh_scr[r - 1] = ext_scr[r:r + rows, :]

    def taps(g):
        cols = slice(g * LANES, (g + 1) * LANES)
        acc = None
        for j in range(n_taps):
            off = CONV_HALO - (n_taps - 1) + j
            r, base = off % SUBLANES, off - off % SUBLANES
            src = ext_scr[base:base + tm, cols] if r == 0 else sh_scr[r - 1, base:base + tm, cols]
            term = dww_ref[j:j + 1, cols] * src
            acc = term if acc is None else acc + term
        cv_scr[:, cols] = acc + dwb_ref[:, cols]

    def finish():
        a_scr[...] = _ln_swish(cv_scr[...], lnw_ref, lnb_ref)

    return [shifts] + [functools.partial(taps, g) for g in range(dc // LANES)] + [finish]


def _post_tail(a, x_ref, mod, o_ref, zg_ref, gnw_ref, wca_ref, wgo_ref, wo_ref, n2w_ref,
               w1_ref, w2_ref, fnw_ref, y_ref, *, dff, side_jobs=()):
    side = list(side_jobs)

    def run_side():
        if side:
            side.pop(0)()

    y_a = _dot(a, wca_ref[...])
    run_side()

    gated = []
    for h in range(N_HEADS):
        hs = slice(h * HEAD_DIM, (h + 1) * HEAD_DIM)
        gated.append(_rms(o_ref[:, hs], gnw_ref[...]) * _silu(zg_ref[:, hs].astype(F32)))
    y_b = _dot(jnp.concatenate(gated, axis=1).astype(BF16), wgo_ref[...])
    run_side()

    d = y_a.shape[1]
    merged = zg_ref[:, d:2 * d].astype(F32) * y_a + zg_ref[:, 2 * d:3 * d].astype(F32) * y_b
    x1 = x_ref[...] + mod(2) * _dot(merged.astype(BF16), wo_ref[...])
    run_side()

    h2 = (_rms(x1, n2w_ref[...]) * (1.0 + mod(4)) + mod(3)).astype(BF16)
    step = 1024
    ff = None
    for c in range(dff // step):
        f = jnp.maximum(_dot(h2, w1_ref[:, c * step:(c + 1) * step]), 0.0)
        part = _dot((f * f).astype(BF16), w2_ref[c * step:(c + 1) * step, :])
        ff = part if ff is None else ff + part
        run_side()
    while side:
        run_side()
    x2 = x1 + mod(5) * ff
    y_ref[...] = _rms(x2, fnw_ref[...])


def _post_prompt_kernel(x_ref, mod_ref, glu_ref, glun_ref, dww_ref, dwb_ref, lnw_ref, lnb_ref,
                        o_ref, zg_ref, gnw_ref, wca_ref, wgo_ref, wo_ref, n2w_ref,
                        w1_ref, w2_ref, fnw_ref, y_ref, ext_scr, sh_scr, cv_scr, a_scr,
                        *, tm, tiles_per_seq, d, dc, dff):
    i = pl.program_id(0)
    conv_jobs = functools.partial(_conv_ln_swish_jobs, ext_scr, sh_scr, cv_scr, a_scr, dww_ref, dwb_ref,
                                  lnw_ref, lnb_ref, tm=tm, dc=dc)

    @pl.when(i == 0)
    def _():
        ext_scr[0:CONV_HALO, :] = jnp.zeros((CONV_HALO, dc), F32)
        ext_scr[CONV_HALO:CONV_HALO + tm, :] = glu_ref[...]
        for job in conv_jobs():
            job()

    a = a_scr[...]
    ext_scr[0:CONV_HALO, :] = jnp.where((i + 1) % tiles_per_seq == 0, 0.0, glu_ref[tm - CONV_HALO:tm, :])
    ext_scr[CONV_HALO:CONV_HALO + tm, :] = glun_ref[...]

    mod = lambda k: mod_ref[0, :, k * d:(k + 1) * d]
    _post_tail(a, x_ref, mod, o_ref, zg_ref, gnw_ref, wca_ref, wgo_ref, wo_ref, n2w_ref,
               w1_ref, w2_ref, fnw_ref, y_ref, dff=dff, side_jobs=conv_jobs())


def _post_sample_kernel(x_ref, mod_ref, apre_ref, lnw_ref, lnb_ref,
                        o_ref, zg_ref, gnw_ref, wca_ref, wgo_ref, wo_ref, n2w_ref,
                        w1_ref, w2_ref, fnw_ref, y_ref, *, d, dff):
    a = _ln_swish(apre_ref[...], lnw_ref, lnb_ref)
    mod = lambda k: mod_ref[:, k * d:(k + 1) * d]
    _post_tail(a, x_ref, mod, o_ref, zg_ref, gnw_ref, wca_ref, wgo_ref, wo_ref, n2w_ref,
               w1_ref, w2_ref, fnw_ref, y_ref, dff=dff)


def _post_call(x, mod, conv_in, lnw, lnb, o, zg, gnw, wca, wgo, wo, n2w, w1, w2, fnw, *, seq_len):
    t, d = x.shape
    dc = lnw.shape[1]
    dff = w1.shape[1]
    prompt = len(conv_in) == 3
    tm = min(POST_TILE, t)
    assert t % tm == 0
    n_tiles = t // tm
    tok = lambda w: pl.BlockSpec((tm, w), lambda i: (i, 0))
    common_args = [lnw, lnb, o, zg, gnw, wca, wgo, wo, n2w, w1, w2, fnw]
    common_specs = [_resident(lnw.shape), _resident(lnb.shape), tok(d), tok(3 * d), _resident(gnw.shape),
                    _resident(wca.shape), _resident(wgo.shape), _resident(wo.shape), _resident(n2w.shape),
                    _resident(w1.shape), _resident(w2.shape), _resident(fnw.shape)]
    if prompt:
        glu, dww, dwb = conv_in
        assert seq_len % tm == 0 and tm >= CONV_HALO >= dww.shape[0] - 1
        tiles_per_seq = seq_len // tm
        args = [x, mod, glu, glu, dww, dwb] + common_args
        specs = [tok(d), pl.BlockSpec((1, 1, mod.shape[2]), lambda i: (i // tiles_per_seq, 0, 0)),
                 tok(dc), pl.BlockSpec((tm, dc), lambda i: (jnp.minimum(i + 1, n_tiles - 1), 0)),
                 _resident(dww.shape), _resident(dwb.shape)] + common_specs
        scratch = [pltpu.VMEM((tm + CONV_HALO, dc), F32),
                   pltpu.VMEM((SUBLANES - 1, tm + CONV_HALO - SUBLANES, dc), F32),
                   pltpu.VMEM((tm, dc), F32),
                   pltpu.VMEM((tm, dc), BF16)]
        kern = functools.partial(_post_prompt_kernel, tm=tm, tiles_per_seq=tiles_per_seq, d=d, dc=dc, dff=dff)
        name = "post_prompt"
    else:
        args = [x, mod, conv_in[0]] + common_args
        specs = [tok(d), tok(mod.shape[1]), tok(dc)] + common_specs
        scratch = []
        kern = functools.partial(_post_sample_kernel, d=d, dff=dff)
        name = "post_sample"
    return pl.pallas_call(
        kern,
        grid=(n_tiles,),
        in_specs=specs,
        out_specs=tok(d),
        out_shape=jax.ShapeDtypeStruct((t, d), F32),
        scratch_shapes=scratch,
        compiler_params=_params(1),
        name=name,
    )(*args)


def _split_w_in(w_in, d, dc, dqkv):
    w_bf = w_in.astype(BF16)
    o1 = 2 * dc + dqkv + d
    ba = w_bf[:, o1:o1 + 2 * N_HEADS]
    gates = w_bf[:, o1 + 2 * N_HEADS:]
    pad = jnp.zeros((w_bf.shape[0], LANES - 2 * N_HEADS), BF16)
    return w_bf, jnp.concatenate([gates, ba, pad], axis=1)


def _lane_pad(vec, offset):
    out = jnp.zeros((1, LANES), F32)
    return out.at[0, offset:offset + vec.shape[0]].set(vec.astype(F32))


def kernel(x_prompt, x_sample, c_prompt, c_sample, state_conf_conv, state_qkv_conv, state_delta, w_ada, b_ada, norm1_w, w_in, conf_dw_w, conf_dw_b, conf_ln_w, conf_ln_b, w_conf_out, gdn_conv_w, a_log, dt_bias, gdn_norm_w, w_gdn_out, w_o, norm2_w, w_ff1, w_ff2, final_norm_w):
    nb, seq_len, d = x_prompt.shape
    ns = x_sample.shape[0]
    depth = w_ada.shape[0]
    dc = conf_dw_w.shape[2]
    dqkv = gdn_conv_w.shape[2]
    n_hist = conf_dw_w.shape[1] - 1
    n_taps = gdn_conv_w.shape[1]
    assert x_sample.shape[1] == 1

    xp = x_prompt.reshape(nb * seq_len, d)
    xs = x_sample.reshape(ns, d)
    c_all = jnp.concatenate([c_sample, c_prompt], axis=0)
    row2 = lambda v: v.reshape(1, -1).astype(F32)

    conf_p, qkv_p, delta_p, conf_s, qkv_s, delta_s = [], [], [], [], [], []
    for l in range(depth):
        mod = _mod_call(c_all, w_ada[l].astype(F32), row2(b_ada[l]))
        mod_p = mod[ns:].reshape(nb, 1, N_MOD * d)
        mod_s = mod
        w_bf, w_tail = _split_w_in(w_in[l], d, dc, dqkv)
        n1w, n2w, fnw = row2(norm1_w[l]), row2(norm2_w[l]), row2(final_norm_w)
        alog_p = _lane_pad(a_log[l], N_HEADS)
        dtb_p = _lane_pad(dt_bias[l], N_HEADS)
        cw = gdn_conv_w[l].astype(F32)
        dww, dwb = conf_dw_w[l].astype(F32), row2(conf_dw_b[l])
        lnw, lnb = row2(conf_ln_w[l]), row2(conf_ln_b[l])
        gnw = row2(gdn_norm_w[l])
        wca, wgo, wo = w_conf_out[l].astype(BF16), w_gdn_out[l].astype(BF16), w_o[l].astype(BF16)
        w1, w2 = w_ff1[l].astype(BF16), w_ff2[l].astype(BF16)
        assert depth == 1

        glu, qkv, zg, bg, bgt, tail = _pre_prompt_call(
            xp, mod_p, n1w, w_bf, w_tail, cw, alog_p, dtb_p, seq_len, dc)
        o, s_fin = _gdn_prompt_call(qkv, bg, bgt, nb, seq_len)
        xp = _post_call(xp, mod_p, (glu, dww, dwb), lnw, lnb, o, zg, gnw, wca, wgo, wo, n2w, w1, w2, fnw,
                        seq_len=seq_len)
        conf_p.append(glu.reshape(nb, seq_len, dc)[:, seq_len - n_hist:])
        qkv_p.append(tail.reshape(nb, SUBLANES, dqkv)[:, SUBLANES - (n_taps - 1):])
        delta_p.append(s_fin)

        sconf = state_conf_conv[l].reshape(ns, n_hist * dc)
        sqkv = state_qkv_conv[l].reshape(ns, (n_taps - 1) * dqkv)
        apre, qkv, zg, bg, nconf, nqkv = _pre_sample_call(
            xs, mod_s, n1w, w_bf, w_tail, cw, alog_p, dtb_p, sconf, sqkv, dww, dwb)
        o, s_new = _gdn_sample_call(qkv, bg, state_delta[l].astype(F32))
        xs = _post_call(xs, mod_s, (apre,), lnw, lnb, o, zg, gnw, wca, wgo, wo, n2w, w1, w2, fnw,
                        seq_len=1)
        conf_s.append(nconf.reshape(ns, n_hist, dc))
        qkv_s.append(nqkv.reshape(ns, n_taps - 1, dqkv))
        delta_s.append(s_new)

    return (xp.reshape(nb, seq_len, d), xs.reshape(ns, 1, d),
            jnp.stack(conf_p), jnp.stack(qkv_p), jnp.stack(delta_p),
            jnp.stack(conf_s), jnp.stack(qkv_s), jnp.stack(delta_s))
```

```python
import functools

import jax
import jax.numpy as jnp
from jax import lax
from jax.experimental import pallas as pl
from jax.experimental.pallas import tpu as pltpu

F32 = jnp.float32
BF16 = jnp.bfloat16

EPS = 1e-6
N_MOD = 6
N_HEADS = 8
HEAD_DIM = 128
CHUNK = 64
LANES = 128
SUBLANES = 8
VMEM_LIMIT_BYTES = 56 * 1024 * 1024

PRE_TILE = 256
POST_TILE = 256
GDN_WINDOW = 128
GDN_WINDOWS_PER_STEP = 4
CONV_HALO = 32
SAMPLE_SEQ_BLOCK = 16
CONV_PHASES = 4


NEG_LOG2_E = -1.4426950408889634


def _sigmoid(x):
    return 1.0 / (1.0 + jnp.exp2(x * NEG_LOG2_E))


def _silu(x):
    return x * _sigmoid(x)


def _softplus(x):
    return jnp.maximum(x, 0.0) + jnp.log(1.0 + jnp.exp(-jnp.abs(x)))


def _dot(a, b):
    return jnp.dot(a, b, preferred_element_type=F32)


def _dot_nt(a, b):
    return lax.dot_general(a, b, (((1,), (1,)), ((), ())), preferred_element_type=F32)


def _resident(shape):
    n = len(shape)
    return pl.BlockSpec(shape, lambda *_: (0,) * n, pipeline_mode=pl.Buffered(1))


def _params(n_grid):
    return pltpu.CompilerParams(dimension_semantics=("arbitrary",) * n_grid,
                                vmem_limit_bytes=VMEM_LIMIT_BYTES)


def _mod_kernel(c_ref, w_ref, b_ref, o_ref):
    c = c_ref[...]
    o_ref[...] = _dot(_silu(c).astype(BF16), w_ref[...].astype(BF16)) + b_ref[...]


def _mod_call(c_all, w_ada, b_ada):
    n, d = c_all.shape
    e = w_ada.shape[1]
    tn = e // 8
    return pl.pallas_call(
        _mod_kernel,
        grid=(e // tn,),
        in_specs=[pl.BlockSpec((n, d), lambda j: (0, 0)),
                  pl.BlockSpec((d, tn), lambda j: (0, j)),
                  pl.BlockSpec((1, tn), lambda j: (0, j))],
        out_specs=pl.BlockSpec((n, tn), lambda j: (0, j)),
        out_shape=jax.ShapeDtypeStruct((n, e), F32),
        compiler_params=_params(1),
        name="mod",
    )(c_all, w_ada, b_ada)


def _pre_front(x, shift1, scale1, n1w):
    ms = jnp.mean(x * x, axis=-1, keepdims=True)
    h = x * lax.rsqrt(ms + EPS) * n1w
    return (h * (1.0 + scale1) + shift1).astype(BF16)


def _qkv_finish(y, g, n_groups):
    s = _silu(y)
    n_qk = n_groups // 3
    if g < 2 * n_qk:
        s = s * lax.rsqrt(jnp.sum(s * s, axis=-1, keepdims=True) + EPS)
    if g < n_qk:
        s = s * (HEAD_DIM ** -0.5)
    return s


def _beta_decay(ba, alog, dtb):
    lane = lax.broadcasted_iota(jnp.int32, ba.shape, 1)
    beta = _sigmoid(ba)
    g = -jnp.exp(alog) * _softplus(ba + dtb)
    return jnp.where(lane < N_HEADS, beta, g)


def _pre_prompt_kernel(x_ref, mod_ref, n1w_ref, w_ref, wt_ref, cw_ref, alog_ref, dtb_ref, cum_ref,
                       glu_ref, qkv_ref, zg_ref, bg_ref, bgt_ref, tail_ref,
                       qkv_scr, h_scr, *, tm, tiles_per_seq, d, dc, dqkv):
    i = pl.program_id(0)
    h_scr[...] = _pre_front(x_ref[...], mod_ref[0, :, 0:d], mod_ref[0, :, d:2 * d], n1w_ref[...])

    u = _dot(h_scr[...], w_ref[:, 0:2 * dc])
    glu_ref[...] = u[:, :dc] * _sigmoid(u[:, dc:])

    @pl.when(i % tiles_per_seq == 0)
    def _():
        qkv_scr[:, 0:SUBLANES, :] = jnp.zeros((dqkv // LANES, SUBLANES, LANES), F32)

    o_qkv = 2 * dc
    o_z = o_qkv + dqkv
    step = 512
    n_chunks = dqkv // step
    n_taps = cw_ref.shape[0]

    n_groups = dqkv // LANES
    groups_per_chunk = step // LANES
    rows_per_phase = tm // CONV_PHASES

    def project(c):
        r = _dot(h_scr[...], w_ref[:, o_qkv + c * step:o_qkv + (c + 1) * step])
        for k in range(groups_per_chunk):
            g = c * groups_per_chunk + k
            qkv_scr[g, SUBLANES:SUBLANES + tm, :] = r[:, k * LANES:(k + 1) * LANES]
            tail_ref[:, g * LANES:(g + 1) * LANES] = qkv_scr[g, tm:tm + SUBLANES, :]

    def conv(c):
        for g in range(c * groups_per_chunk, (c + 1) * groups_per_chunk):
            cols = slice(g * LANES, (g + 1) * LANES)
            first = SUBLANES - (n_taps - 1)
            slabs = {start: qkv_scr[g, pl.ds(start, rows_per_phase, stride=CONV_PHASES), :]
                     for start in range(first, first + n_taps + CONV_PHASES - 1)}
            for p in range(CONV_PHASES):
                y = None
                for j in range(n_taps):
                    term = cw_ref[j:j + 1, cols] * slabs[first + j + p]
                    y = term if y is None else y + term
                qkv_ref[g, pl.ds(p, rows_per_phase, stride=CONV_PHASES), :] = _qkv_finish(y, g, n_groups)
            qkv_scr[g, 0:SUBLANES, :] = qkv_scr[g, tm:tm + SUBLANES, :]

    pieces = [(wsrc, o_w + half * step, k * d + half * step, act)
              for k, (wsrc, o_w, act) in enumerate([(w_ref, o_z, False), (wt_ref, 0, True), (wt_ref, d, True)])
              for half in range(d // step)]
    assert len(pieces) == n_chunks

    def side(c):
        wsrc, o_w, o_out, act = pieces[c]
        r = _dot(h_scr[...], wsrc[:, o_w:o_w + step])
        zg_ref[:, o_out:o_out + step] = (_sigmoid(r) if act else r).astype(BF16)

    project(0)
    for c in range(n_chunks):
        if c + 1 < n_chunks:
            project(c + 1)
        side(c)
        conv(c)

    ba = _dot(h_scr[...], wt_ref[:, 2 * d:2 * d + LANES])
    bg = _beta_decay(ba, alog_ref[...], dtb_ref[...])
    bg_hi = bg.astype(BF16)
    bg_lo = (bg - bg_hi.astype(F32)).astype(BF16)
    cum = _dot(cum_ref[...], bg_hi) + _dot(cum_ref[...], bg_lo)
    lane = lax.broadcasted_iota(jnp.int32, bg.shape, 1)
    bg = jnp.where(lane < N_HEADS, bg, cum)
    bg_ref[...] = bg
    bgt_ref[...] = bg.T


def _pre_prompt_call(x, mod_p, n1w, w_bf, w_tail, cw, alog_p, dtb_p, seq_len, dc):
    t, d = x.shape
    tm = PRE_TILE
    assert seq_len % tm == 0 and tm % CHUNK == 0 and tm % (CONV_PHASES * SUBLANES) == 0
    dqkv = cw.shape[1]
    w_main_shape = (d, 2 * dc + dqkv + d)
    nb = t // seq_len
    tiles_per_seq = seq_len // tm
    row = lax.broadcasted_iota(jnp.int32, (tm, tm), 0)
    col = lax.broadcasted_iota(jnp.int32, (tm, tm), 1)
    cum_mat = ((row // CHUNK == col // CHUNK) & (col <= row)).astype(BF16)

    tok = lambda w: pl.BlockSpec((tm, w), lambda i: (i, 0))
    kern = functools.partial(_pre_prompt_kernel, tm=tm, tiles_per_seq=tiles_per_seq, d=d, dc=dc, dqkv=dqkv)
    return pl.pallas_call(
        kern,
        grid=(t // tm,),
        in_specs=[tok(d),
                  pl.BlockSpec((1, 1, mod_p.shape[2]), lambda i: (i // tiles_per_seq, 0, 0)),
                  _resident(n1w.shape), _resident(w_main_shape), _resident(w_tail.shape), _resident(cw.shape),
                  _resident(alog_p.shape), _resident(dtb_p.shape), _resident(cum_mat.shape)],
        out_specs=[tok(dc), pl.BlockSpec((dqkv // LANES, tm, LANES), lambda i: (0, i, 0)), tok(3 * d), tok(LANES),
                   pl.BlockSpec((LANES, tm), lambda i: (0, i)),
                   pl.BlockSpec((SUBLANES, dqkv), lambda i: (i // tiles_per_seq, 0))],
        out_shape=[jax.ShapeDtypeStruct((t, dc), F32),
                   jax.ShapeDtypeStruct((dqkv // LANES, t, LANES), F32), jax.ShapeDtypeStruct((t, 3 * d), BF16),
                   jax.ShapeDtypeStruct((t, LANES), F32),
                   jax.ShapeDtypeStruct((LANES, t), F32),
                   jax.ShapeDtypeStruct((nb * SUBLANES, dqkv), F32)],
        scratch_shapes=[pltpu.VMEM((dqkv // LANES, tm + SUBLANES, LANES), F32), pltpu.VMEM((tm, d), BF16)],
        compiler_params=_params(1),
        name="pre_prompt",
    )(x, mod_p, n1w, w_bf, w_tail, cw, alog_p, dtb_p, cum_mat)


def _pre_sample_kernel(x_ref, mod_ref, n1w_ref, w_ref, wt_ref, cw_ref, alog_ref, dtb_ref,
                       sconf_ref, sqkv_ref, dww_ref, dwb_ref,
                       apre_ref, qkv_ref, zg_ref, bg_ref,
                       nconf_ref, nqkv_ref, *, d, dc, dqkv):
    hb = _pre_front(x_ref[...], mod_ref[:, 0:d], mod_ref[:, d:2 * d], n1w_ref[...])

    u = _dot(hb, w_ref[:, 0:2 * dc])
    glu = u[:, :dc] * _sigmoid(u[:, dc:])
    n_hist = sconf_ref.shape[1] // dc
    acc = dww_ref[n_hist:n_hist + 1, :] * glu + dwb_ref[...]
    for j in range(n_hist):
        acc = acc + dww_ref[j:j + 1, :] * sconf_ref[:, j * dc:(j + 1) * dc]
    apre_ref[...] = acc
    nconf_ref[:, 0:(n_hist - 1) * dc] = sconf_ref[:, dc:n_hist * dc]
    nconf_ref[:, (n_hist - 1) * dc:n_hist * dc] = glu

    o_qkv = 2 * dc
    n_taps = cw_ref.shape[0]
    nqkv_ref[:, 0:(n_taps - 2) * dqkv] = sqkv_ref[:, dqkv:(n_taps - 1) * dqkv]
    for g in range(dqkv // LANES):
        cols = slice(g * LANES, (g + 1) * LANES)
        raw = _dot(hb, w_ref[:, o_qkv + g * LANES:o_qkv + (g + 1) * LANES])
        nqkv_ref[:, (n_taps - 2) * dqkv + g * LANES:(n_taps - 2) * dqkv + (g + 1) * LANES] = raw
        y = cw_ref[n_taps - 1:n_taps, cols] * raw
        for j in range(n_taps - 1):
            y = y + cw_ref[j:j + 1, cols] * sqkv_ref[:, j * dqkv + g * LANES:j * dqkv + (g + 1) * LANES]
        qkv_ref[g] = _qkv_finish(y, g, dqkv // LANES)

    o_z = o_qkv + dqkv
    zg_ref[:, 0:d] = _dot(hb, w_ref[:, o_z:o_z + d]).astype(BF16)
    zg_ref[:, d:2 * d] = _sigmoid(_dot(hb, wt_ref[:, 0:d])).astype(BF16)
    zg_ref[:, 2 * d:3 * d] = _sigmoid(_dot(hb, wt_ref[:, d:2 * d])).astype(BF16)
    ba = _dot(hb, wt_ref[:, 2 * d:2 * d + LANES])
    bg_ref[...] = _beta_decay(ba, alog_ref[...], dtb_ref[...])


def _pre_sample_call(x, mod_s, n1w, w_bf, w_tail, cw, alog_p, dtb_p, sconf, sqkv, dww, dwb):
    n, d = x.shape
    dqkv = cw.shape[1]
    dc = dwb.shape[1]
    w_main_shape = (d, 2 * dc + dqkv + d)
    tm = 32
    assert n % tm == 0
    tok = lambda w: pl.BlockSpec((tm, w), lambda i: (i, 0))
    kern = functools.partial(_pre_sample_kernel, d=d, dc=dc, dqkv=dqkv)
    return pl.pallas_call(
        kern,
        grid=(n // tm,),
        in_specs=[tok(d), tok(mod_s.shape[1]),
                  _resident(n1w.shape), _resident(w_main_shape), _resident(w_tail.shape), _resident(cw.shape),
                  _resident(alog_p.shape), _resident(dtb_p.shape),
                  tok(sconf.shape[1]), tok(sqkv.shape[1]),
                  _resident(dww.shape), _resident(dwb.shape)],
        out_specs=[tok(dc), pl.BlockSpec((dqkv // LANES, tm, LANES), lambda i: (0, i, 0)), tok(3 * d), tok(LANES),
                   tok(sconf.shape[1]), tok(sqkv.shape[1])],
        out_shape=[jax.ShapeDtypeStruct((n, dc), F32),
                   jax.ShapeDtypeStruct((dqkv // LANES, n, LANES), F32), jax.ShapeDtypeStruct((n, 3 * d), BF16),
                   jax.ShapeDtypeStruct((n, LANES), F32),
                   jax.ShapeDtypeStruct(sconf.shape, F32), jax.ShapeDtypeStruct(sqkv.shape, F32)],
        compiler_params=_params(1),
        name="pre_sample",
    )(x, mod_s, n1w, w_bf, w_tail, cw, alog_p, dtb_p, sconf, sqkv, dww, dwb)


def _half_block_mask(row, col, size):
    shift = size.bit_length() - 1
    half = size // 2
    return ((lax.shift_right_logical(row, shift) == lax.shift_right_logical(col, shift))
            & ((row & half) != 0) & ((col & half) == 0))


def _block_diag2(a, b):
    z = jnp.zeros(a.shape, a.dtype)
    return jnp.concatenate([jnp.concatenate([a, z], axis=1), jnp.concatenate([z, b], axis=1)], axis=0)


def _pair_dot(a, b):
    n = b[0].shape[1]
    out = _dot(jnp.concatenate(a, axis=1), _block_diag2(b[0], b[1]))
    return [out[:, :n], out[:, n:]]


def _gdn_prompt_kernel(q_ref, k_ref, v_ref, bg_ref, bgt_ref, o_ref, sfin_ref, s_scr, *, w, nw, n_steps):
    step = pl.program_id(1)

    @pl.when(step == 0)
    def _():
        s_scr[...] = jnp.zeros(s_scr.shape, F32)

    nc = w // CHUNK
    heads = range(N_HEADS)
    units = [(win, h) for win in range(nw) for h in heads]
    n_units = len(units)
    upairs = range(0, n_units, 2)
    row = lax.broadcasted_iota(jnp.int32, (w, w), 0)
    col = lax.broadcasted_iota(jnp.int32, (w, w), 1)
    shift = CHUNK.bit_length() - 1
    same = lax.shift_right_logical(row, shift) == lax.shift_right_logical(col, shift)
    m_incl = same & (col <= row)
    m_strict = same & (col < row)
    eye = (row == col).astype(F32)

    rs = [slice(win * w, (win + 1) * w) for win, _ in units]
    hs = [slice(h * HEAD_DIM, (h + 1) * HEAD_DIM) for _, h in units]
    qh = [q_ref[h, rs[u], :] for u, (_, h) in enumerate(units)]
    kh = [k_ref[h, rs[u], :] for u, (_, h) in enumerate(units)]
    vh = [v_ref[h, rs[u], :] for u, (_, h) in enumerate(units)]
    bcol, gcb, decay = [], [], []
    for u, (win, h) in enumerate(units):
        bcol.append(jnp.broadcast_to(bg_ref[rs[u], h:h + 1], (w, HEAD_DIM)))
        gcb.append(jnp.broadcast_to(bg_ref[rs[u], N_HEADS + h:N_HEADS + h + 1], (w, HEAD_DIM)))
        grow = bgt_ref[N_HEADS + h:N_HEADS + h + 1, rs[u]]
        decay.append(jnp.exp(jnp.where(m_incl, gcb[u] - grow, -1e30)))

    kb = [kh[u].astype(BF16) for u in range(n_units)]
    qb = [qh[u].astype(BF16) for u in range(n_units)]
    amat, att = [None] * n_units, [None] * n_units
    for p in upairs:
        lhs = jnp.concatenate([jnp.concatenate([kb[p], kb[p + 1]], axis=1),
                               jnp.concatenate([qb[p], qb[p + 1]], axis=1)], axis=0)
        r = _dot_nt(lhs, _block_diag2(kb[p], kb[p + 1]))
        for j in range(2):
            kk = r[:w, j * w:(j + 1) * w]
            amat[p + j] = jnp.where(m_strict, kk * bcol[p + j] * decay[p + j], 0.0)
            att[p + j] = (r[w:, j * w:(j + 1) * w] * decay[p + j]).astype(BF16)

    x = [eye - jnp.where(_half_block_mask(row, col, 2), amat[u], 0.0) for u in range(n_units)]
    size = 4
    while size <= CHUNK:
        mask = _half_block_mask(row, col, size)
        for p in upairs:
            blk = [jnp.where(mask, amat[p + j], 0.0).astype(BF16) for j in range(2)]
            xb = [x[p + j].astype(BF16) for j in range(2)]
            y = _pair_dot(blk, xb)
            upd = _pair_dot(xb, [y[j].astype(BF16) for j in range(2)])
            for j in range(2):
                x[p + j] = x[p + j] - upd[j]
        size *= 2

    eg = [jnp.exp(gcb[u]) for u in range(n_units)]
    kv = []
    for u in range(n_units):
        rhs = jnp.concatenate([kh[u] * (bcol[u] * eg[u]), vh[u] * bcol[u]], axis=1).astype(BF16)
        kv.append(_dot(x[u].astype(BF16), rhs).astype(BF16))
    qe, glast, ktt = [], [], []
    for u in range(n_units):
        qe.append(qh[u] * eg[u])
        gl = jnp.concatenate(
            [jnp.broadcast_to(gcb[u][(c + 1) * CHUNK - 1:(c + 1) * CHUNK, :], (CHUNK, HEAD_DIM)) for c in range(nc)],
            axis=0)
        glast.append(gl)
        ktt.append((kh[u] * jnp.exp(gl - gcb[u])).T.astype(BF16))
    wmat = [[None] * nc for _ in range(n_units)]
    bmat = [[None] * nc for _ in range(n_units)]
    qmat = [[None] * nc for _ in range(n_units)]
    omat = [[None] * nc for _ in range(n_units)]
    zeros = jnp.zeros((CHUNK, 2 * HEAD_DIM), BF16)
    for c in range(nc):
        cs = slice(c * CHUNK, (c + 1) * CHUNK)
        for u in range(n_units):
            parts = [zeros] * nc
            parts[c] = kv[u][cs]
            kv_c = jnp.concatenate(parts, axis=0)
            r = _dot(jnp.concatenate([ktt[u], att[u][cs]], axis=0), kv_c)
            wmat[u][c] = r[:HEAD_DIM, :HEAD_DIM]
            bmat[u][c] = r[:HEAD_DIM, HEAD_DIM:]
            qmat[u][c] = qe[u][cs] - r[HEAD_DIM:, :HEAD_DIM]
            omat[u][c] = r[HEAD_DIM:, HEAD_DIM:]

    s = [s_scr[h] for h in heads]
    for win in range(nw):
        for c in range(nc):
            cs = slice(win * w + c * CHUNK, win * w + (c + 1) * CHUNK)
            for hp in range(0, N_HEADS, 2):
                u = win * N_HEADS + hp
                lhs = [jnp.concatenate([wmat[u + j][c], qmat[u + j][c]], axis=0).astype(BF16) for j in range(2)]
                r = _pair_dot(lhs, [s[hp + j].astype(BF16) for j in range(2)])
                for j in range(2):
                    o_ref[cs, hs[u + j]] = r[j][HEAD_DIM:] + omat[u + j][c]
                    decay_c = jnp.exp(glast[u + j][c * CHUNK:c * CHUNK + 1, :])
                    s[hp + j] = s[hp + j] * decay_c - r[j][:HEAD_DIM] + bmat[u + j][c]
    for h in heads:
        s_scr[h] = s[h]

    @pl.when(step == n_steps - 1)
    def _():
        sfin_ref[0] = s_scr[...]


def _gdn_prompt_call(qkv, bg, bgt, nb, seq_len):
    t, d = qkv.shape[1], N_HEADS * HEAD_DIM
    w = GDN_WINDOW
    nw = GDN_WINDOWS_PER_STEP
    assert seq_len % (w * nw) == 0 and w % CHUNK == 0 and w == HEAD_DIM
    n_steps = seq_len // (w * nw)
    tok = pl.BlockSpec((w * nw, d), lambda b, i: (b * n_steps + i, 0))
    part = lambda k: pl.BlockSpec((N_HEADS, w * nw, HEAD_DIM), lambda b, i: (k, b * n_steps + i, 0))
    kern = functools.partial(_gdn_prompt_kernel, w=w, nw=nw, n_steps=n_steps)
    return pl.pallas_call(
        kern,
        grid=(nb, n_steps),
        in_specs=[part(0), part(1), part(2),
                  pl.BlockSpec((w * nw, LANES), lambda b, i: (b * n_steps + i, 0)),
                  pl.BlockSpec((LANES, w * nw), lambda b, i: (0, b * n_steps + i))],
        out_specs=[tok, pl.BlockSpec((1, N_HEADS, HEAD_DIM, HEAD_DIM), lambda b, i: (b, 0, 0, 0))],
        out_shape=[jax.ShapeDtypeStruct((t, d), F32),
                   jax.ShapeDtypeStruct((nb, N_HEADS, HEAD_DIM, HEAD_DIM), F32)],
        scratch_shapes=[pltpu.VMEM((N_HEADS, HEAD_DIM, HEAD_DIM), F32)],
        compiler_params=_params(2),
        name="gdn_prompt",
    )(qkv, qkv, qkv, bg, bgt)


def _gdn_sample_kernel(q_ref, k_ref, v_ref, bg_ref, s_ref, o_ref, snew_ref, *, bs):
    row = lax.broadcasted_iota(jnp.int32, (HEAD_DIM, HEAD_DIM), 0)
    col = lax.broadcasted_iota(jnp.int32, (HEAD_DIM, HEAD_DIM), 1)
    eye = row == col
    for h in range(N_HEADS):
        hs = slice(h * HEAD_DIM, (h + 1) * HEAD_DIM)
        qh = q_ref[h]
        kh = k_ref[h]
        vh = v_ref[h]
        beta = jnp.broadcast_to(bg_ref[:, h:h + 1], (bs, HEAD_DIM))
        eg = jnp.exp(jnp.broadcast_to(bg_ref[:, N_HEADS + h:N_HEADS + h + 1], (bs, HEAD_DIM)))
        att = jnp.sum(qh * kh, axis=-1, keepdims=True)
        for i in range(bs):
            s = s_ref[i, h]
            kcol = jnp.sum(jnp.where(eye, jnp.broadcast_to(kh[i:i + 1, :], (HEAD_DIM, HEAD_DIM)), 0.0),
                           axis=-1, keepdims=True)
            qcol = jnp.sum(jnp.where(eye, jnp.broadcast_to(qh[i:i + 1, :], (HEAD_DIM, HEAD_DIM)), 0.0),
                           axis=-1, keepdims=True)
            ks = jnp.sum(kcol * s, axis=0, keepdims=True)
            qs = jnp.sum(qcol * s, axis=0, keepdims=True)
            egi = eg[i:i + 1, :]
            v_new = beta[i:i + 1, :] * (vh[i:i + 1, :] - egi * ks)
            o_ref[i:i + 1, hs] = egi * qs + att[i:i + 1, :] * v_new
            snew_ref[i, h] = s * egi + kcol * v_new


def _gdn_sample_call(qkv, bg, s0):
    n, d = qkv.shape[1], N_HEADS * HEAD_DIM
    bs = SAMPLE_SEQ_BLOCK
    assert n % bs == 0
    tok = lambda w: pl.BlockSpec((bs, w), lambda i: (i, 0))
    st = pl.BlockSpec((bs, N_HEADS, HEAD_DIM, HEAD_DIM), lambda i: (i, 0, 0, 0))
    return pl.pallas_call(
        functools.partial(_gdn_sample_kernel, bs=bs),
        grid=(n // bs,),
        in_specs=[pl.BlockSpec((N_HEADS, bs, HEAD_DIM), lambda i: (0, i, 0)),
                  pl.BlockSpec((N_HEADS, bs, HEAD_DIM), lambda i: (1, i, 0)),
                  pl.BlockSpec((N_HEADS, bs, HEAD_DIM), lambda i: (2, i, 0)), tok(LANES), st],
        out_specs=[tok(d), st],
        out_shape=[jax.ShapeDtypeStruct((n, d), F32), jax.ShapeDtypeStruct(s0.shape, F32)],
        compiler_params=_params(1),
        name="gdn_sample",
    )(qkv, qkv, qkv, bg, s0)


def _rms(x, w):
    return x * lax.rsqrt(jnp.mean(x * x, axis=-1, keepdims=True) + EPS) * w


def _ln_swish(a, lnw_ref, lnb_ref):
    mu = jnp.mean(a, axis=-1, keepdims=True)
    ac = a - mu
    var = jnp.mean(ac * ac, axis=-1, keepdims=True)
    return _silu(ac * lax.rsqrt(var + EPS) * lnw_ref[...] + lnb_ref[...]).astype(BF16)


def _conv_ln_swish_jobs(ext_scr, cv_scr, a_scr, dww_ref, dwb_ref, lnw_ref, lnb_ref, *, tm, dc):
    n_taps = dww_ref.shape[0]
    rows_per_phase = tm // CONV_PHASES
    first = CONV_HALO - (n_taps - 1)

    def taps(g):
        cols = slice(g * LANES, (g + 1) * LANES)
        acc = [None] * CONV_PHASES
        for start in range(first, first + n_taps + CONV_PHASES - 1):
            src = ext_scr[g, pl.ds(start, rows_per_phase, stride=CONV_PHASES), :]
            for p in range(CONV_PHASES):
                j = start - first - p
                if 0 <= j < n_taps:
                    term = dww_ref[j:j + 1, cols] * src
                    acc[p] = term if acc[p] is None else acc[p] + term
        for p in range(CONV_PHASES):
            cv_scr[g, pl.ds(p, rows_per_phase, stride=CONV_PHASES), :] = acc[p] + dwb_ref[:, cols]

    def finish():
        a = jnp.concatenate([cv_scr[g] for g in range(dc // LANES)], axis=1)
        a_scr[...] = _ln_swish(a, lnw_ref, lnb_ref)

    return [functools.partial(taps, g) for g in range(dc // LANES)] + [finish]


def _fill_conv_history(ext_scr, hist, new, *, tm, dc):
    for g in range(dc // LANES):
        cols = slice(g * LANES, (g + 1) * LANES)
        ext_scr[g, 0:CONV_HALO, :] = hist[:, cols]
        ext_scr[g, CONV_HALO:CONV_HALO + tm, :] = new[:, cols]


def _post_tail(a, x_ref, mod, o_ref, zg_ref, gnw_ref, wca_ref, wgo_ref, wo_ref, n2w_ref,
               w1_ref, w2_ref, fnw_ref, y_ref, *, dff, side_jobs=()):
    side = list(side_jobs)

    def run_side():
        if side:
            side.pop(0)()

    y_a = _dot(a, wca_ref[...])
    run_side()

    gated = []
    for h in range(N_HEADS):
        hs = slice(h * HEAD_DIM, (h + 1) * HEAD_DIM)
        gated.append(_rms(o_ref[:, hs], gnw_ref[...]) * _silu(zg_ref[:, hs].astype(F32)))
    y_b = _dot(jnp.concatenate(gated, axis=1).astype(BF16), wgo_ref[...])
    run_side()

    d = y_a.shape[1]
    merged = zg_ref[:, d:2 * d].astype(F32) * y_a + zg_ref[:, 2 * d:3 * d].astype(F32) * y_b
    x1 = x_ref[...] + mod(2) * _dot(merged.astype(BF16), wo_ref[...])
    run_side()

    h2 = (_rms(x1, n2w_ref[...]) * (1.0 + mod(4)) + mod(3)).astype(BF16)
    step = 1024
    ff = None
    for c in range(dff // step):
        f = jnp.maximum(_dot(h2, w1_ref[:, c * step:(c + 1) * step]), 0.0)
        part = _dot((f * f).astype(BF16), w2_ref[c * step:(c + 1) * step, :])
        ff = part if ff is None else ff + part
        run_side()
    while side:
        run_side()
    x2 = x1 + mod(5) * ff
    y_ref[...] = _rms(x2, fnw_ref[...])


def _post_prompt_kernel(x_ref, mod_ref, glu_ref, glun_ref, dww_ref, dwb_ref, lnw_ref, lnb_ref,
                        o_ref, zg_ref, gnw_ref, wca_ref, wgo_ref, wo_ref, n2w_ref,
                        w1_ref, w2_ref, fnw_ref, y_ref, ext_scr, cv_scr, a_scr,
                        *, tm, tiles_per_seq, d, dc, dff):
    i = pl.program_id(0)
    conv_jobs = functools.partial(_conv_ln_swish_jobs, ext_scr, cv_scr, a_scr, dww_ref, dwb_ref,
                                  lnw_ref, lnb_ref, tm=tm, dc=dc)

    @pl.when(i == 0)
    def _():
        _fill_conv_history(ext_scr, jnp.zeros((CONV_HALO, dc), F32), glu_ref[...], tm=tm, dc=dc)
        for job in conv_jobs():
            job()

    a = a_scr[...]
    hist = jnp.where((i + 1) % tiles_per_seq == 0, 0.0, glu_ref[tm - CONV_HALO:tm, :])
    _fill_conv_history(ext_scr, hist, glun_ref[...], tm=tm, dc=dc)

    mod = lambda k: mod_ref[0, :, k * d:(k + 1) * d]
    _post_tail(a, x_ref, mod, o_ref, zg_ref, gnw_ref, wca_ref, wgo_ref, wo_ref, n2w_ref,
               w1_ref, w2_ref, fnw_ref, y_ref, dff=dff, side_jobs=conv_jobs())


def _post_sample_kernel(x_ref, mod_ref, apre_ref, lnw_ref, lnb_ref,
                        o_ref, zg_ref, gnw_ref, wca_ref, wgo_ref, wo_ref, n2w_ref,
                        w1_ref, w2_ref, fnw_ref, y_ref, *, d, dff):
    a = _ln_swish(apre_ref[...], lnw_ref, lnb_ref)
    mod = lambda k: mod_ref[:, k * d:(k + 1) * d]
    _post_tail(a, x_ref, mod, o_ref, zg_ref, gnw_ref, wca_ref, wgo_ref, wo_ref, n2w_ref,
               w1_ref, w2_ref, fnw_ref, y_ref, dff=dff)


def _post_call(x, mod, conv_in, lnw, lnb, o, zg, gnw, wca, wgo, wo, n2w, w1, w2, fnw, *, seq_len):
    t, d = x.shape
    dc = lnw.shape[1]
    dff = w1.shape[1]
    prompt = len(conv_in) == 3
    tm = min(POST_TILE, t)
    assert t % tm == 0
    n_tiles = t // tm
    tok = lambda w: pl.BlockSpec((tm, w), lambda i: (i, 0))
    common_args = [lnw, lnb, o, zg, gnw, wca, wgo, wo, n2w, w1, w2, fnw]
    common_specs = [_resident(lnw.shape), _resident(lnb.shape), tok(d), tok(3 * d), _resident(gnw.shape),
                    _resident(wca.shape), _resident(wgo.shape), _resident(wo.shape), _resident(n2w.shape),
                    _resident(w1.shape), _resident(w2.shape), _resident(fnw.shape)]
    if prompt:
        glu, dww, dwb = conv_in
        assert seq_len % tm == 0 and tm >= CONV_HALO >= dww.shape[0] - 1
        tiles_per_seq = seq_len // tm
        args = [x, mod, glu, glu, dww, dwb] + common_args
        specs = [tok(d), pl.BlockSpec((1, 1, mod.shape[2]), lambda i: (i // tiles_per_seq, 0, 0)),
                 tok(dc), pl.BlockSpec((tm, dc), lambda i: (jnp.minimum(i + 1, n_tiles - 1), 0)),
                 _resident(dww.shape), _resident(dwb.shape)] + common_specs
        scratch = [pltpu.VMEM((dc // LANES, tm + CONV_HALO, LANES), F32),
                   pltpu.VMEM((dc // LANES, tm, LANES), F32),
                   pltpu.VMEM((tm, dc), BF16)]
        kern = functools.partial(_post_prompt_kernel, tm=tm, tiles_per_seq=tiles_per_seq, d=d, dc=dc, dff=dff)
        name = "post_prompt"
    else:
        args = [x, mod, conv_in[0]] + common_args
        specs = [tok(d), tok(mod.shape[1]), tok(dc)] + common_specs
        scratch = []
        kern = functools.partial(_post_sample_kernel, d=d, dff=dff)
        name = "post_sample"
    return pl.pallas_call(
        kern,
        grid=(n_tiles,),
        in_specs=specs,
        out_specs=tok(d),
        out_shape=jax.ShapeDtypeStruct((t, d), F32),
        scratch_shapes=scratch,
        compiler_params=_params(1),
        name=name,
    )(*args)


def _split_w_in(w_in, d, dc, dqkv):
    w_bf = w_in.astype(BF16)
    o1 = 2 * dc + dqkv + d
    ba = w_bf[:, o1:o1 + 2 * N_HEADS]
    gates = w_bf[:, o1 + 2 * N_HEADS:]
    pad = jnp.zeros((w_bf.shape[0], LANES - 2 * N_HEADS), BF16)
    return w_bf, jnp.concatenate([gates, ba, pad], axis=1)


def _lane_pad(vec, offset):
    out = jnp.zeros((1, LANES), F32)
    return out.at[0, offset:offset + vec.shape[0]].set(vec.astype(F32))


def kernel(x_prompt, x_sample, c_prompt, c_sample, state_conf_conv, state_qkv_conv, state_delta, w_ada, b_ada, norm1_w, w_in, conf_dw_w, conf_dw_b, conf_ln_w, conf_ln_b, w_conf_out, gdn_conv_w, a_log, dt_bias, gdn_norm_w, w_gdn_out, w_o, norm2_w, w_ff1, w_ff2, final_norm_w):
    nb, seq_len, d = x_prompt.shape
    ns = x_sample.shape[0]
    depth = w_ada.shape[0]
    dc = conf_dw_w.shape[2]
    dqkv = gdn_conv_w.shape[2]
    n_hist = conf_dw_w.shape[1] - 1
    n_taps = gdn_conv_w.shape[1]
    assert x_sample.shape[1] == 1

    xp = x_prompt.reshape(nb * seq_len, d)
    xs = x_sample.reshape(ns, d)
    c_all = jnp.concatenate([c_sample, c_prompt], axis=0)
    row2 = lambda v: v.reshape(1, -1).astype(F32)

    conf_p, qkv_p, delta_p, conf_s, qkv_s, delta_s = [], [], [], [], [], []
    for l in range(depth):
        mod = _mod_call(c_all, w_ada[l].astype(F32), row2(b_ada[l]))
        mod_p = mod[ns:].reshape(nb, 1, N_MOD * d)
        mod_s = mod
        w_bf, w_tail = _split_w_in(w_in[l], d, dc, dqkv)
        n1w, n2w, fnw = row2(norm1_w[l]), row2(norm2_w[l]), row2(final_norm_w)
        alog_p = _lane_pad(a_log[l], N_HEADS)
        dtb_p = _lane_pad(dt_bias[l], N_HEADS)
        cw = gdn_conv_w[l].astype(F32)
        dww, dwb = conf_dw_w[l].astype(F32), row2(conf_dw_b[l])
        lnw, lnb = row2(conf_ln_w[l]), row2(conf_ln_b[l])
        gnw = row2(gdn_norm_w[l])
        wca, wgo, wo = w_conf_out[l].astype(BF16), w_gdn_out[l].astype(BF16), w_o[l].astype(BF16)
        w1, w2 = w_ff1[l].astype(BF16), w_ff2[l].astype(BF16)
        assert depth == 1

        glu, qkv, zg, bg, bgt, tail = _pre_prompt_call(
            xp, mod_p, n1w, w_bf, w_tail, cw, alog_p, dtb_p, seq_len, dc)
        o, s_fin = _gdn_prompt_call(qkv, bg, bgt, nb, seq_len)
        xp = _post_call(xp, mod_p, (glu, dww, dwb), lnw, lnb, o, zg, gnw, wca, wgo, wo, n2w, w1, w2, fnw,
                        seq_len=seq_len)
        conf_p.append(glu.reshape(nb, seq_len, dc)[:, seq_len - n_hist:])
        qkv_p.append(tail.reshape(nb, SUBLANES, dqkv)[:, SUBLANES - (n_taps - 1):])
        delta_p.append(s_fin)

        sconf = state_conf_conv[l].reshape(ns, n_hist * dc)
        sqkv = state_qkv_conv[l].reshape(ns, (n_taps - 1) * dqkv)
        apre, qkv, zg, bg, nconf, nqkv = _pre_sample_call(
            xs, mod_s, n1w, w_bf, w_tail, cw, alog_p, dtb_p, sconf, sqkv, dww, dwb)
        o, s_new = _gdn_sample_call(qkv, bg, state_delta[l].astype(F32))
        xs = _post_call(xs, mod_s, (apre,), lnw, lnb, o, zg, gnw, wca, wgo, wo, n2w, w1, w2, fnw,
                        seq_len=1)
        conf_s.append(nconf.reshape(ns, n_hist, dc))
        qkv_s.append(nqkv.reshape(ns, n_taps - 1, dqkv))
        delta_s.append(s_new)

    return (xp.reshape(nb, seq_len, d), xs.reshape(ns, 1, d),
            jnp.stack(conf_p), jnp.stack(qkv_p), jnp.stack(delta_p),
            jnp.stack(conf_s), jnp.stack(qkv_s), jnp.stack(delta_s))
```

```python
import functools

import jax
import jax.numpy as jnp
from jax import lax
from jax.experimental import pallas as pl
from jax.experimental.pallas import tpu as pltpu

F32 = jnp.float32
BF16 = jnp.bfloat16

EPS = 1e-6
N_MOD = 6
N_HEADS = 8
HEAD_DIM = 128
CHUNK = 64
LANES = 128
SUBLANES = 8
VMEM_LIMIT_BYTES = 56 * 1024 * 1024

PRE_TILE = 256
POST_TILE = 256
GDN_WINDOW = 128
GDN_WINDOWS_PER_STEP = 4
CONV_HALO = 32
SAMPLE_SEQ_BLOCK = 16
CONV_PHASES = 4


NEG_LOG2_E = -1.4426950408889634


def _sigmoid(x):
    return 1.0 / (1.0 + jnp.exp2(x * NEG_LOG2_E))


def _silu(x):
    return x * _sigmoid(x)


def _softplus(x):
    return jnp.maximum(x, 0.0) + jnp.log(1.0 + jnp.exp(-jnp.abs(x)))


def _dot(a, b):
    return jnp.dot(a, b, preferred_element_type=F32)


def _dot_nt(a, b):
    return lax.dot_general(a, b, (((1,), (1,)), ((), ())), preferred_element_type=F32)


def _resident(shape):
    n = len(shape)
    return pl.BlockSpec(shape, lambda *_: (0,) * n, pipeline_mode=pl.Buffered(1))


def _params(n_grid):
    return pltpu.CompilerParams(dimension_semantics=("arbitrary",) * n_grid,
                                vmem_limit_bytes=VMEM_LIMIT_BYTES)


def _mod_kernel(c_ref, w_ref, b_ref, o_ref):
    c = c_ref[...]
    o_ref[...] = _dot(_silu(c).astype(BF16), w_ref[...].astype(BF16)) + b_ref[...]


def _mod_call(c_all, w_ada, b_ada):
    n, d = c_all.shape
    e = w_ada.shape[1]
    tn = e // 8
    return pl.pallas_call(
        _mod_kernel,
        grid=(e // tn,),
        in_specs=[pl.BlockSpec((n, d), lambda j: (0, 0)),
                  pl.BlockSpec((d, tn), lambda j: (0, j)),
                  pl.BlockSpec((1, tn), lambda j: (0, j))],
        out_specs=pl.BlockSpec((n, tn), lambda j: (0, j)),
        out_shape=jax.ShapeDtypeStruct((n, e), F32),
        compiler_params=_params(1),
        name="mod",
    )(c_all, w_ada, b_ada)


def _pre_front(x, shift1, scale1, n1w):
    ms = jnp.mean(x * x, axis=-1, keepdims=True)
    h = x * lax.rsqrt(ms + EPS) * n1w
    return (h * (1.0 + scale1) + shift1).astype(BF16)


def _qkv_finish(y, g, n_groups):
    s = _silu(y)
    n_qk = n_groups // 3
    if g < 2 * n_qk:
        s = s * lax.rsqrt(jnp.sum(s * s, axis=-1, keepdims=True) + EPS)
    if g < n_qk:
        s = s * (HEAD_DIM ** -0.5)
    return s


def _beta_decay(ba, alog, dtb):
    lane = lax.broadcasted_iota(jnp.int32, ba.shape, 1)
    beta = _sigmoid(ba)
    g = -jnp.exp(alog) * _softplus(ba + dtb)
    return jnp.where(lane < N_HEADS, beta, g)


def _pre_prompt_kernel(x_ref, mod_ref, n1w_ref, w_ref, wt_ref, cw_ref, alog_ref, dtb_ref, cum_ref,
                       glu_ref, qkv_ref, zg_ref, bg_ref, bgt_ref, tail_ref,
                       qkv_scr, h_scr, *, tm, tiles_per_seq, d, dc, dqkv):
    i = pl.program_id(0)
    h_scr[...] = _pre_front(x_ref[...], mod_ref[0, :, 0:d], mod_ref[0, :, d:2 * d], n1w_ref[...])

    u = _dot(h_scr[...], w_ref[:, 0:2 * dc])
    glu_ref[...] = u[:, :dc] * _sigmoid(u[:, dc:])

    @pl.when(i % tiles_per_seq == 0)
    def _():
        qkv_scr[:, 0:SUBLANES, :] = jnp.zeros((dqkv // LANES, SUBLANES, LANES), F32)

    o_qkv = 2 * dc
    o_z = o_qkv + dqkv
    step = 512
    n_chunks = dqkv // step
    n_taps = cw_ref.shape[0]

    n_groups = dqkv // LANES
    groups_per_chunk = step // LANES
    rows_per_phase = tm // CONV_PHASES

    def project(c):
        r = _dot(h_scr[...], w_ref[:, o_qkv + c * step:o_qkv + (c + 1) * step])
        for k in range(groups_per_chunk):
            g = c * groups_per_chunk + k
            qkv_scr[g, SUBLANES:SUBLANES + tm, :] = r[:, k * LANES:(k + 1) * LANES]
            tail_ref[:, g * LANES:(g + 1) * LANES] = qkv_scr[g, tm:tm + SUBLANES, :]

    def conv(c):
        for g in range(c * groups_per_chunk, (c + 1) * groups_per_chunk):
            cols = slice(g * LANES, (g + 1) * LANES)
            first = SUBLANES - (n_taps - 1)
            slabs = {start: qkv_scr[g, pl.ds(start, rows_per_phase, stride=CONV_PHASES), :]
                     for start in range(first, first + n_taps + CONV_PHASES - 1)}
            for p in range(CONV_PHASES):
                y = None
                for j in range(n_taps):
                    term = cw_ref[j:j + 1, cols] * slabs[first + j + p]
                    y = term if y is None else y + term
                qkv_ref[g, pl.ds(p, rows_per_phase, stride=CONV_PHASES), :] = _qkv_finish(y, g, n_groups)
            qkv_scr[g, 0:SUBLANES, :] = qkv_scr[g, tm:tm + SUBLANES, :]

    pieces = [(wsrc, o_w + half * step, k * d + half * step, act)
              for k, (wsrc, o_w, act) in enumerate([(w_ref, o_z, False), (wt_ref, 0, True), (wt_ref, d, True)])
              for half in range(d // step)]
    assert len(pieces) == n_chunks

    def side(c):
        wsrc, o_w, o_out, act = pieces[c]
        r = _dot(h_scr[...], wsrc[:, o_w:o_w + step])
        zg_ref[:, o_out:o_out + step] = (_sigmoid(r) if act else r).astype(BF16)

    project(0)
    for c in range(n_chunks):
        if c + 1 < n_chunks:
            project(c + 1)
        side(c)
        conv(c)

    ba = _dot(h_scr[...], wt_ref[:, 2 * d:2 * d + LANES])
    bg = _beta_decay(ba, alog_ref[...], dtb_ref[...])
    bg_hi = bg.astype(BF16)
    bg_lo = (bg - bg_hi.astype(F32)).astype(BF16)
    cum = _dot(cum_ref[...], bg_hi) + _dot(cum_ref[...], bg_lo)
    lane = lax.broadcasted_iota(jnp.int32, bg.shape, 1)
    bg = jnp.where(lane < N_HEADS, bg, cum)
    bg_ref[...] = bg
    bgt_ref[...] = bg.T


def _pre_prompt_call(x, mod_p, n1w, w_bf, w_tail, cw, alog_p, dtb_p, seq_len, dc):
    t, d = x.shape
    tm = PRE_TILE
    assert seq_len % tm == 0 and tm % CHUNK == 0 and tm % (CONV_PHASES * SUBLANES) == 0
    dqkv = cw.shape[1]
    w_main_shape = (d, 2 * dc + dqkv + d)
    nb = t // seq_len
    tiles_per_seq = seq_len // tm
    row = lax.broadcasted_iota(jnp.int32, (tm, tm), 0)
    col = lax.broadcasted_iota(jnp.int32, (tm, tm), 1)
    cum_mat = ((row // CHUNK == col // CHUNK) & (col <= row)).astype(BF16)

    tok = lambda w: pl.BlockSpec((tm, w), lambda i: (i, 0))
    kern = functools.partial(_pre_prompt_kernel, tm=tm, tiles_per_seq=tiles_per_seq, d=d, dc=dc, dqkv=dqkv)
    return pl.pallas_call(
        kern,
        grid=(t // tm,),
        in_specs=[tok(d),
                  pl.BlockSpec((1, 1, mod_p.shape[2]), lambda i: (i // tiles_per_seq, 0, 0)),
                  _resident(n1w.shape), _resident(w_main_shape), _resident(w_tail.shape), _resident(cw.shape),
                  _resident(alog_p.shape), _resident(dtb_p.shape), _resident(cum_mat.shape)],
        out_specs=[tok(dc), pl.BlockSpec((dqkv // LANES, tm, LANES), lambda i: (0, i, 0)), tok(3 * d), tok(LANES),
                   pl.BlockSpec((LANES, tm), lambda i: (0, i)),
                   pl.BlockSpec((SUBLANES, dqkv), lambda i: (i // tiles_per_seq, 0))],
        out_shape=[jax.ShapeDtypeStruct((t, dc), F32),
                   jax.ShapeDtypeStruct((dqkv // LANES, t, LANES), F32), jax.ShapeDtypeStruct((t, 3 * d), BF16),
                   jax.ShapeDtypeStruct((t, LANES), F32),
                   jax.ShapeDtypeStruct((LANES, t), F32),
                   jax.ShapeDtypeStruct((nb * SUBLANES, dqkv), F32)],
        scratch_shapes=[pltpu.VMEM((dqkv // LANES, tm + SUBLANES, LANES), F32), pltpu.VMEM((tm, d), BF16)],
        compiler_params=_params(1),
        name="pre_prompt",
    )(x, mod_p, n1w, w_bf, w_tail, cw, alog_p, dtb_p, cum_mat)


def _pre_sample_kernel(x_ref, mod_ref, n1w_ref, w_ref, wt_ref, cw_ref, alog_ref, dtb_ref,
                       sconf_ref, sqkv_ref, dww_ref, dwb_ref,
                       apre_ref, qkv_ref, zg_ref, bg_ref,
                       nconf_ref, nqkv_ref, *, d, dc, dqkv):
    hb = _pre_front(x_ref[...], mod_ref[:, 0:d], mod_ref[:, d:2 * d], n1w_ref[...])

    u = _dot(hb, w_ref[:, 0:2 * dc])
    glu = u[:, :dc] * _sigmoid(u[:, dc:])
    n_hist = sconf_ref.shape[0]
    acc = dww_ref[n_hist:n_hist + 1, :] * glu + dwb_ref[...]
    for j in range(n_hist):
        row = sconf_ref[j]
        acc = acc + dww_ref[j:j + 1, :] * row
        if j > 0:
            nconf_ref[j - 1] = row
    apre_ref[...] = acc
    nconf_ref[n_hist - 1] = glu

    o_qkv = 2 * dc
    n_taps = cw_ref.shape[0]
    for g in range(dqkv // LANES):
        cols = slice(g * LANES, (g + 1) * LANES)
        raw = _dot(hb, w_ref[:, o_qkv + g * LANES:o_qkv + (g + 1) * LANES])
        nqkv_ref[n_taps - 2, :, cols] = raw
        y = cw_ref[n_taps - 1:n_taps, cols] * raw
        for j in range(n_taps - 1):
            row = sqkv_ref[j, :, cols]
            y = y + cw_ref[j:j + 1, cols] * row
            if j > 0:
                nqkv_ref[j - 1, :, cols] = row
        qkv_ref[g] = _qkv_finish(y, g, dqkv // LANES)

    o_z = o_qkv + dqkv
    zg_ref[:, 0:d] = _dot(hb, w_ref[:, o_z:o_z + d]).astype(BF16)
    zg_ref[:, d:2 * d] = _sigmoid(_dot(hb, wt_ref[:, 0:d])).astype(BF16)
    zg_ref[:, 2 * d:3 * d] = _sigmoid(_dot(hb, wt_ref[:, d:2 * d])).astype(BF16)
    ba = _dot(hb, wt_ref[:, 2 * d:2 * d + LANES])
    bg_ref[...] = _beta_decay(ba, alog_ref[...], dtb_ref[...])


def _pre_sample_call(x, mod_s, n1w, w_bf, w_tail, cw, alog_p, dtb_p, sconf, sqkv, dww, dwb):
    n, d = x.shape
    dqkv = cw.shape[1]
    dc = dwb.shape[1]
    w_main_shape = (d, 2 * dc + dqkv + d)
    tm = 32
    assert n % tm == 0
    tok = lambda w: pl.BlockSpec((tm, w), lambda i: (i, 0))
    state = lambda a: pl.BlockSpec((a.shape[0], tm, a.shape[2]), lambda i: (0, i, 0))
    kern = functools.partial(_pre_sample_kernel, d=d, dc=dc, dqkv=dqkv)
    return pl.pallas_call(
        kern,
        grid=(n // tm,),
        in_specs=[tok(d), tok(mod_s.shape[1]),
                  _resident(n1w.shape), _resident(w_main_shape), _resident(w_tail.shape), _resident(cw.shape),
                  _resident(alog_p.shape), _resident(dtb_p.shape),
                  state(sconf), state(sqkv),
                  _resident(dww.shape), _resident(dwb.shape)],
        out_specs=[tok(dc), pl.BlockSpec((dqkv // LANES, tm, LANES), lambda i: (0, i, 0)), tok(3 * d), tok(LANES),
                   state(sconf), state(sqkv)],
        out_shape=[jax.ShapeDtypeStruct((n, dc), F32),
                   jax.ShapeDtypeStruct((dqkv // LANES, n, LANES), F32), jax.ShapeDtypeStruct((n, 3 * d), BF16),
                   jax.ShapeDtypeStruct((n, LANES), F32),
                   jax.ShapeDtypeStruct(sconf.shape, F32), jax.ShapeDtypeStruct(sqkv.shape, F32)],
        compiler_params=_params(1),
        name="pre_sample",
    )(x, mod_s, n1w, w_bf, w_tail, cw, alog_p, dtb_p, sconf, sqkv, dww, dwb)


def _half_block_mask(row, col, size):
    shift = size.bit_length() - 1
    half = size // 2
    return ((lax.shift_right_logical(row, shift) == lax.shift_right_logical(col, shift))
            & ((row & half) != 0) & ((col & half) == 0))


def _block_diag2(a, b):
    z = jnp.zeros(a.shape, a.dtype)
    return jnp.concatenate([jnp.concatenate([a, z], axis=1), jnp.concatenate([z, b], axis=1)], axis=0)


def _pair_dot(a, b):
    n = b[0].shape[1]
    out = _dot(jnp.concatenate(a, axis=1), _block_diag2(b[0], b[1]))
    return [out[:, :n], out[:, n:]]


def _gdn_prompt_kernel(q_ref, k_ref, v_ref, bg_ref, bgt_ref, o_ref, sfin_ref, s_scr, *, w, nw, n_steps):
    step = pl.program_id(1)

    @pl.when(step == 0)
    def _():
        s_scr[...] = jnp.zeros(s_scr.shape, F32)

    nc = w // CHUNK
    heads = range(N_HEADS)
    units = [(win, h) for win in range(nw) for h in heads]
    n_units = len(units)
    upairs = range(0, n_units, 2)
    row = lax.broadcasted_iota(jnp.int32, (w, w), 0)
    col = lax.broadcasted_iota(jnp.int32, (w, w), 1)
    shift = CHUNK.bit_length() - 1
    same = lax.shift_right_logical(row, shift) == lax.shift_right_logical(col, shift)
    m_incl = same & (col <= row)
    m_strict = same & (col < row)
    eye = (row == col).astype(F32)

    rs = [slice(win * w, (win + 1) * w) for win, _ in units]
    hs = [slice(h * HEAD_DIM, (h + 1) * HEAD_DIM) for _, h in units]
    qh = [q_ref[h, rs[u], :] for u, (_, h) in enumerate(units)]
    kh = [k_ref[h, rs[u], :] for u, (_, h) in enumerate(units)]
    vh = [v_ref[h, rs[u], :] for u, (_, h) in enumerate(units)]
    bcol, gcb, decay = [], [], []
    for u, (win, h) in enumerate(units):
        bcol.append(jnp.broadcast_to(bg_ref[rs[u], h:h + 1], (w, HEAD_DIM)))
        gcb.append(jnp.broadcast_to(bg_ref[rs[u], N_HEADS + h:N_HEADS + h + 1], (w, HEAD_DIM)))
        grow = bgt_ref[N_HEADS + h:N_HEADS + h + 1, rs[u]]
        decay.append(jnp.exp(jnp.where(m_incl, gcb[u] - grow, -1e30)))

    kb = [kh[u].astype(BF16) for u in range(n_units)]
    qb = [qh[u].astype(BF16) for u in range(n_units)]
    amat, att = [None] * n_units, [None] * n_units
    for p in upairs:
        lhs = jnp.concatenate([jnp.concatenate([kb[p], kb[p + 1]], axis=1),
                               jnp.concatenate([qb[p], qb[p + 1]], axis=1)], axis=0)
        r = _dot_nt(lhs, _block_diag2(kb[p], kb[p + 1]))
        for j in range(2):
            kk = r[:w, j * w:(j + 1) * w]
            amat[p + j] = jnp.where(m_strict, kk * bcol[p + j] * decay[p + j], 0.0)
            att[p + j] = (r[w:, j * w:(j + 1) * w] * decay[p + j]).astype(BF16)

    x = [eye - jnp.where(_half_block_mask(row, col, 2), amat[u], 0.0) for u in range(n_units)]
    size = 4
    while size <= CHUNK:
        mask = _half_block_mask(row, col, size)
        for p in upairs:
            blk = [jnp.where(mask, amat[p + j], 0.0).astype(BF16) for j in range(2)]
            xb = [x[p + j].astype(BF16) for j in range(2)]
            y = _pair_dot(blk, xb)
            upd = _pair_dot(xb, [y[j].astype(BF16) for j in range(2)])
            for j in range(2):
                x[p + j] = x[p + j] - upd[j]
        size *= 2

    eg = [jnp.exp(gcb[u]) for u in range(n_units)]
    kv = []
    for u in range(n_units):
        rhs = jnp.concatenate([kh[u] * (bcol[u] * eg[u]), vh[u] * bcol[u]], axis=1).astype(BF16)
        kv.append(_dot(x[u].astype(BF16), rhs).astype(BF16))
    qe, glast, ktt = [], [], []
    for u in range(n_units):
        qe.append(qh[u] * eg[u])
        gl = jnp.concatenate(
            [jnp.broadcast_to(gcb[u][(c + 1) * CHUNK - 1:(c + 1) * CHUNK, :], (CHUNK, HEAD_DIM)) for c in range(nc)],
            axis=0)
        glast.append(gl)
        ktt.append((kh[u] * jnp.exp(gl - gcb[u])).T.astype(BF16))
    wmat = [[None] * nc for _ in range(n_units)]
    bmat = [[None] * nc for _ in range(n_units)]
    qmat = [[None] * nc for _ in range(n_units)]
    omat = [[None] * nc for _ in range(n_units)]
    zeros = jnp.zeros((CHUNK, 2 * HEAD_DIM), BF16)
    for c in range(nc):
        cs = slice(c * CHUNK, (c + 1) * CHUNK)
        for u in range(n_units):
            parts = [zeros] * nc
            parts[c] = kv[u][cs]
            kv_c = jnp.concatenate(parts, axis=0)
            r = _dot(jnp.concatenate([ktt[u], att[u][cs]], axis=0), kv_c)
            wmat[u][c] = r[:HEAD_DIM, :HEAD_DIM]
            bmat[u][c] = r[:HEAD_DIM, HEAD_DIM:]
            qmat[u][c] = qe[u][cs] - r[HEAD_DIM:, :HEAD_DIM]
            omat[u][c] = r[HEAD_DIM:, HEAD_DIM:]

    s = [s_scr[h] for h in heads]
    for win in range(nw):
        for c in range(nc):
            cs = slice(win * w + c * CHUNK, win * w + (c + 1) * CHUNK)
            for hp in range(0, N_HEADS, 2):
                u = win * N_HEADS + hp
                lhs = [jnp.concatenate([wmat[u + j][c], qmat[u + j][c]], axis=0).astype(BF16) for j in range(2)]
                r = _pair_dot(lhs, [s[hp + j].astype(BF16) for j in range(2)])
                for j in range(2):
                    o_ref[cs, hs[u + j]] = r[j][HEAD_DIM:] + omat[u + j][c]
                    decay_c = jnp.exp(glast[u + j][c * CHUNK:c * CHUNK + 1, :])
                    s[hp + j] = s[hp + j] * decay_c - r[j][:HEAD_DIM] + bmat[u + j][c]
    for h in heads:
        s_scr[h] = s[h]

    @pl.when(step == n_steps - 1)
    def _():
        sfin_ref[0] = s_scr[...]


def _gdn_prompt_call(qkv, bg, bgt, nb, seq_len):
    t, d = qkv.shape[1], N_HEADS * HEAD_DIM
    w = GDN_WINDOW
    nw = GDN_WINDOWS_PER_STEP
    assert seq_len % (w * nw) == 0 and w % CHUNK == 0 and w == HEAD_DIM
    n_steps = seq_len // (w * nw)
    tok = pl.BlockSpec((w * nw, d), lambda b, i: (b * n_steps + i, 0))
    part = lambda k: pl.BlockSpec((N_HEADS, w * nw, HEAD_DIM), lambda b, i: (k, b * n_steps + i, 0))
    kern = functools.partial(_gdn_prompt_kernel, w=w, nw=nw, n_steps=n_steps)
    return pl.pallas_call(
        kern,
        grid=(nb, n_steps),
        in_specs=[part(0), part(1), part(2),
                  pl.BlockSpec((w * nw, LANES), lambda b, i: (b * n_steps + i, 0)),
                  pl.BlockSpec((LANES, w * nw), lambda b, i: (0, b * n_steps + i))],
        out_specs=[tok, pl.BlockSpec((1, N_HEADS, HEAD_DIM, HEAD_DIM), lambda b, i: (b, 0, 0, 0))],
        out_shape=[jax.ShapeDtypeStruct((t, d), F32),
                   jax.ShapeDtypeStruct((nb, N_HEADS, HEAD_DIM, HEAD_DIM), F32)],
        scratch_shapes=[pltpu.VMEM((N_HEADS, HEAD_DIM, HEAD_DIM), F32)],
        compiler_params=_params(2),
        name="gdn_prompt",
    )(qkv, qkv, qkv, bg, bgt)


def _gdn_sample_kernel(q_ref, k_ref, v_ref, bg_ref, s_ref, o_ref, snew_ref, *, bs):
    row = lax.broadcasted_iota(jnp.int32, (HEAD_DIM, HEAD_DIM), 0)
    col = lax.broadcasted_iota(jnp.int32, (HEAD_DIM, HEAD_DIM), 1)
    eye = row == col
    for h in range(N_HEADS):
        hs = slice(h * HEAD_DIM, (h + 1) * HEAD_DIM)
        qh = q_ref[h]
        kh = k_ref[h]
        vh = v_ref[h]
        beta = jnp.broadcast_to(bg_ref[:, h:h + 1], (bs, HEAD_DIM))
        eg = jnp.exp(jnp.broadcast_to(bg_ref[:, N_HEADS + h:N_HEADS + h + 1], (bs, HEAD_DIM)))
        att = jnp.sum(qh * kh, axis=-1, keepdims=True)
        for i in range(bs):
            s = s_ref[i, h]
            kcol = jnp.sum(jnp.where(eye, jnp.broadcast_to(kh[i:i + 1, :], (HEAD_DIM, HEAD_DIM)), 0.0),
                           axis=-1, keepdims=True)
            qcol = jnp.sum(jnp.where(eye, jnp.broadcast_to(qh[i:i + 1, :], (HEAD_DIM, HEAD_DIM)), 0.0),
                           axis=-1, keepdims=True)
            ks = jnp.sum(kcol * s, axis=0, keepdims=True)
            qs = jnp.sum(qcol * s, axis=0, keepdims=True)
            egi = eg[i:i + 1, :]
            v_new = beta[i:i + 1, :] * (vh[i:i + 1, :] - egi * ks)
            o_ref[i:i + 1, hs] = egi * qs + att[i:i + 1, :] * v_new
            snew_ref[i, h] = s * egi + kcol * v_new


def _gdn_sample_call(qkv, bg, s0):
    n, d = qkv.shape[1], N_HEADS * HEAD_DIM
    bs = SAMPLE_SEQ_BLOCK
    assert n % bs == 0
    tok = lambda w: pl.BlockSpec((bs, w), lambda i: (i, 0))
    st = pl.BlockSpec((bs, N_HEADS, HEAD_DIM, HEAD_DIM), lambda i: (i, 0, 0, 0))
    return pl.pallas_call(
        functools.partial(_gdn_sample_kernel, bs=bs),
        grid=(n // bs,),
        in_specs=[pl.BlockSpec((N_HEADS, bs, HEAD_DIM), lambda i: (0, i, 0)),
                  pl.BlockSpec((N_HEADS, bs, HEAD_DIM), lambda i: (1, i, 0)),
                  pl.BlockSpec((N_HEADS, bs, HEAD_DIM), lambda i: (2, i, 0)), tok(LANES), st],
        out_specs=[tok(d), st],
        out_shape=[jax.ShapeDtypeStruct((n, d), F32), jax.ShapeDtypeStruct(s0.shape, F32)],
        compiler_params=_params(1),
        name="gdn_sample",
    )(qkv, qkv, qkv, bg, s0)


def _rms(x, w):
    return x * lax.rsqrt(jnp.mean(x * x, axis=-1, keepdims=True) + EPS) * w


def _ln_swish(a, lnw_ref, lnb_ref):
    mu = jnp.mean(a, axis=-1, keepdims=True)
    ac = a - mu
    var = jnp.mean(ac * ac, axis=-1, keepdims=True)
    return _silu(ac * lax.rsqrt(var + EPS) * lnw_ref[...] + lnb_ref[...]).astype(BF16)


def _conv_ln_swish_jobs(ext_scr, cv_scr, a_scr, dww_ref, dwb_ref, lnw_ref, lnb_ref, *, tm, dc):
    n_taps = dww_ref.shape[0]
    rows_per_phase = tm // CONV_PHASES
    first = CONV_HALO - (n_taps - 1)

    def taps(g):
        cols = slice(g * LANES, (g + 1) * LANES)
        acc = [None] * CONV_PHASES
        for start in range(first, first + n_taps + CONV_PHASES - 1):
            src = ext_scr[g, pl.ds(start, rows_per_phase, stride=CONV_PHASES), :]
            for p in range(CONV_PHASES):
                j = start - first - p
                if 0 <= j < n_taps:
                    term = dww_ref[j:j + 1, cols] * src
                    acc[p] = term if acc[p] is None else acc[p] + term
        for p in range(CONV_PHASES):
            cv_scr[g, pl.ds(p, rows_per_phase, stride=CONV_PHASES), :] = acc[p] + dwb_ref[:, cols]

    def finish():
        a = jnp.concatenate([cv_scr[g] for g in range(dc // LANES)], axis=1)
        a_scr[...] = _ln_swish(a, lnw_ref, lnb_ref)

    return [functools.partial(taps, g) for g in range(dc // LANES)] + [finish]


def _fill_conv_history(ext_scr, hist, new, *, tm, dc):
    for g in range(dc // LANES):
        cols = slice(g * LANES, (g + 1) * LANES)
        ext_scr[g, 0:CONV_HALO, :] = hist[:, cols]
        ext_scr[g, CONV_HALO:CONV_HALO + tm, :] = new[:, cols]


def _post_tail(a, x_ref, mod, o_ref, zg_ref, gnw_ref, wca_ref, wgo_ref, wo_ref, n2w_ref,
               w1_ref, w2_ref, fnw_ref, y_ref, *, dff, side_jobs=()):
    side = list(side_jobs)

    def run_side():
        if side:
            side.pop(0)()

    y_a = _dot(a, wca_ref[...])
    run_side()

    gated = []
    for h in range(N_HEADS):
        hs = slice(h * HEAD_DIM, (h + 1) * HEAD_DIM)
        gated.append(_rms(o_ref[:, hs], gnw_ref[...]) * _silu(zg_ref[:, hs].astype(F32)))
    y_b = _dot(jnp.concatenate(gated, axis=1).astype(BF16), wgo_ref[...])
    run_side()

    d = y_a.shape[1]
    merged = zg_ref[:, d:2 * d].astype(F32) * y_a + zg_ref[:, 2 * d:3 * d].astype(F32) * y_b
    x1 = x_ref[...] + mod(2) * _dot(merged.astype(BF16), wo_ref[...])
    run_side()

    h2 = (_rms(x1, n2w_ref[...]) * (1.0 + mod(4)) + mod(3)).astype(BF16)
    step = 1024
    ff = None
    for c in range(dff // step):
        f = jnp.maximum(_dot(h2, w1_ref[:, c * step:(c + 1) * step]), 0.0)
        part = _dot((f * f).astype(BF16), w2_ref[c * step:(c + 1) * step, :])
        ff = part if ff is None else ff + part
        run_side()
    while side:
        run_side()
    x2 = x1 + mod(5) * ff
    y_ref[...] = _rms(x2, fnw_ref[...])


def _post_prompt_kernel(x_ref, mod_ref, glu_ref, glun_ref, dww_ref, dwb_ref, lnw_ref, lnb_ref,
                        o_ref, zg_ref, gnw_ref, wca_ref, wgo_ref, wo_ref, n2w_ref,
                        w1_ref, w2_ref, fnw_ref, y_ref, ext_scr, cv_scr, a_scr,
                        *, tm, tiles_per_seq, d, dc, dff):
    i = pl.program_id(0)
    conv_jobs = functools.partial(_conv_ln_swish_jobs, ext_scr, cv_scr, a_scr, dww_ref, dwb_ref,
                                  lnw_ref, lnb_ref, tm=tm, dc=dc)

    @pl.when(i == 0)
    def _():
        _fill_conv_history(ext_scr, jnp.zeros((CONV_HALO, dc), F32), glu_ref[...], tm=tm, dc=dc)
        for job in conv_jobs():
            job()

    a = a_scr[...]
    hist = jnp.where((i + 1) % tiles_per_seq == 0, 0.0, glu_ref[tm - CONV_HALO:tm, :])
    _fill_conv_history(ext_scr, hist, glun_ref[...], tm=tm, dc=dc)

    mod = lambda k: mod_ref[0, :, k * d:(k + 1) * d]
    _post_tail(a, x_ref, mod, o_ref, zg_ref, gnw_ref, wca_ref, wgo_ref, wo_ref, n2w_ref,
               w1_ref, w2_ref, fnw_ref, y_ref, dff=dff, side_jobs=conv_jobs())


def _post_sample_kernel(x_ref, mod_ref, apre_ref, lnw_ref, lnb_ref,
                        o_ref, zg_ref, gnw_ref, wca_ref, wgo_ref, wo_ref, n2w_ref,
                        w1_ref, w2_ref, fnw_ref, y_ref, *, d, dff):
    a = _ln_swish(apre_ref[...], lnw_ref, lnb_ref)
    mod = lambda k: mod_ref[:, k * d:(k + 1) * d]
    _post_tail(a, x_ref, mod, o_ref, zg_ref, gnw_ref, wca_ref, wgo_ref, wo_ref, n2w_ref,
               w1_ref, w2_ref, fnw_ref, y_ref, dff=dff)


def _post_call(x, mod, conv_in, lnw, lnb, o, zg, gnw, wca, wgo, wo, n2w, w1, w2, fnw, *, seq_len):
    t, d = x.shape
    dc = lnw.shape[1]
    dff = w1.shape[1]
    prompt = len(conv_in) == 3
    tm = min(POST_TILE, t)
    assert t % tm == 0
    n_tiles = t // tm
    tok = lambda w: pl.BlockSpec((tm, w), lambda i: (i, 0))
    common_args = [lnw, lnb, o, zg, gnw, wca, wgo, wo, n2w, w1, w2, fnw]
    common_specs = [_resident(lnw.shape), _resident(lnb.shape), tok(d), tok(3 * d), _resident(gnw.shape),
                    _resident(wca.shape), _resident(wgo.shape), _resident(wo.shape), _resident(n2w.shape),
                    _resident(w1.shape), _resident(w2.shape), _resident(fnw.shape)]
    if prompt:
        glu, dww, dwb = conv_in
        assert seq_len % tm == 0 and tm >= CONV_HALO >= dww.shape[0] - 1
        tiles_per_seq = seq_len // tm
        args = [x, mod, glu, glu, dww, dwb] + common_args
        specs = [tok(d), pl.BlockSpec((1, 1, mod.shape[2]), lambda i: (i // tiles_per_seq, 0, 0)),
                 tok(dc), pl.BlockSpec((tm, dc), lambda i: (jnp.minimum(i + 1, n_tiles - 1), 0)),
                 _resident(dww.shape), _resident(dwb.shape)] + common_specs
        scratch = [pltpu.VMEM((dc // LANES, tm + CONV_HALO, LANES), F32),
                   pltpu.VMEM((dc // LANES, tm, LANES), F32),
                   pltpu.VMEM((tm, dc), BF16)]
        kern = functools.partial(_post_prompt_kernel, tm=tm, tiles_per_seq=tiles_per_seq, d=d, dc=dc, dff=dff)
        name = "post_prompt"
    else:
        args = [x, mod, conv_in[0]] + common_args
        specs = [tok(d), tok(mod.shape[1]), tok(dc)] + common_specs
        scratch = []
        kern = functools.partial(_post_sample_kernel, d=d, dff=dff)
        name = "post_sample"
    return pl.pallas_call(
        kern,
        grid=(n_tiles,),
        in_specs=specs,
        out_specs=tok(d),
        out_shape=jax.ShapeDtypeStruct((t, d), F32),
        scratch_shapes=scratch,
        compiler_params=_params(1),
        name=name,
    )(*args)


def _split_w_in(w_in, d, dc, dqkv):
    w_bf = w_in.astype(BF16)
    o1 = 2 * dc + dqkv + d
    ba = w_bf[:, o1:o1 + 2 * N_HEADS]
    gates = w_bf[:, o1 + 2 * N_HEADS:]
    pad = jnp.zeros((w_bf.shape[0], LANES - 2 * N_HEADS), BF16)
    return w_bf, jnp.concatenate([gates, ba, pad], axis=1)


def _lane_pad(vec, offset):
    out = jnp.zeros((1, LANES), F32)
    return out.at[0, offset:offset + vec.shape[0]].set(vec.astype(F32))


def kernel(x_prompt, x_sample, c_prompt, c_sample, state_conf_conv, state_qkv_conv, state_delta, w_ada, b_ada, norm1_w, w_in, conf_dw_w, conf_dw_b, conf_ln_w, conf_ln_b, w_conf_out, gdn_conv_w, a_log, dt_bias, gdn_norm_w, w_gdn_out, w_o, norm2_w, w_ff1, w_ff2, final_norm_w):
    nb, seq_len, d = x_prompt.shape
    ns = x_sample.shape[0]
    depth = w_ada.shape[0]
    dc = conf_dw_w.shape[2]
    dqkv = gdn_conv_w.shape[2]
    n_hist = conf_dw_w.shape[1] - 1
    n_taps = gdn_conv_w.shape[1]
    assert x_sample.shape[1] == 1

    xp = x_prompt.reshape(nb * seq_len, d)
    xs = x_sample.reshape(ns, d)
    c_all = jnp.concatenate([c_sample, c_prompt], axis=0)
    row2 = lambda v: v.reshape(1, -1).astype(F32)

    conf_p, qkv_p, delta_p, conf_s, qkv_s, delta_s = [], [], [], [], [], []
    for l in range(depth):
        mod = _mod_call(c_all, w_ada[l].astype(F32), row2(b_ada[l]))
        mod_p = mod[ns:].reshape(nb, 1, N_MOD * d)
        mod_s = mod
        w_bf, w_tail = _split_w_in(w_in[l], d, dc, dqkv)
        n1w, n2w, fnw = row2(norm1_w[l]), row2(norm2_w[l]), row2(final_norm_w)
        alog_p = _lane_pad(a_log[l], N_HEADS)
        dtb_p = _lane_pad(dt_bias[l], N_HEADS)
        cw = gdn_conv_w[l].astype(F32)
        dww, dwb = conf_dw_w[l].astype(F32), row2(conf_dw_b[l])
        lnw, lnb = row2(conf_ln_w[l]), row2(conf_ln_b[l])
        gnw = row2(gdn_norm_w[l])
        wca, wgo, wo = w_conf_out[l].astype(BF16), w_gdn_out[l].astype(BF16), w_o[l].astype(BF16)
        w1, w2 = w_ff1[l].astype(BF16), w_ff2[l].astype(BF16)
        assert depth == 1

        glu, qkv, zg, bg, bgt, tail = _pre_prompt_call(
            xp, mod_p, n1w, w_bf, w_tail, cw, alog_p, dtb_p, seq_len, dc)
        o, s_fin = _gdn_prompt_call(qkv, bg, bgt, nb, seq_len)
        xp = _post_call(xp, mod_p, (glu, dww, dwb), lnw, lnb, o, zg, gnw, wca, wgo, wo, n2w, w1, w2, fnw,
                        seq_len=seq_len)
        conf_p.append(glu.reshape(nb, seq_len, dc)[:, seq_len - n_hist:])
        qkv_p.append(tail.reshape(nb, SUBLANES, dqkv)[:, SUBLANES - (n_taps - 1):])
        delta_p.append(s_fin)

        sconf = jnp.transpose(state_conf_conv[l].astype(F32), (1, 0, 2))
        sqkv = jnp.transpose(state_qkv_conv[l].astype(F32), (1, 0, 2))
        apre, qkv, zg, bg, nconf, nqkv = _pre_sample_call(
            xs, mod_s, n1w, w_bf, w_tail, cw, alog_p, dtb_p, sconf, sqkv, dww, dwb)
        o, s_new = _gdn_sample_call(qkv, bg, state_delta[l].astype(F32))
        xs = _post_call(xs, mod_s, (apre,), lnw, lnb, o, zg, gnw, wca, wgo, wo, n2w, w1, w2, fnw,
                        seq_len=1)
        conf_s.append(jnp.transpose(nconf, (1, 0, 2)))
        qkv_s.append(jnp.transpose(nqkv, (1, 0, 2)))
        delta_s.append(s_new)

    return (xp.reshape(nb, seq_len, d), xs.reshape(ns, 1, d),
            jnp.stack(conf_p), jnp.stack(qkv_p), jnp.stack(delta_p),
            jnp.stack(conf_s), jnp.stack(qkv_s), jnp.stack(delta_s))
```

```python
import functools

import jax
import jax.numpy as jnp
from jax import lax
from jax.experimental import pallas as pl
from jax.experimental.pallas import tpu as pltpu

F32 = jnp.float32
BF16 = jnp.bfloat16

EPS = 1e-6
N_MOD = 6
N_HEADS = 8
HEAD_DIM = 128
CHUNK = 64
LANES = 128
SUBLANES = 8
VMEM_LIMIT_BYTES = 56 * 1024 * 1024

PRE_TILE = 256
POST_TILE = 256
GDN_WINDOW = 128
GDN_WINDOWS_PER_STEP = 4
CONV_HALO = 32
SAMPLE_SEQ_BLOCK = 16
CONV_PHASES = 4


NEG_LOG2_E = -1.4426950408889634


def _sigmoid(x):
    return 1.0 / (1.0 + jnp.exp2(x * NEG_LOG2_E))


def _silu(x):
    return x * _sigmoid(x)


def _softplus(x):
    return jnp.maximum(x, 0.0) + jnp.log(1.0 + jnp.exp(-jnp.abs(x)))


def _dot(a, b):
    return jnp.dot(a, b, preferred_element_type=F32)


def _dot_nt(a, b):
    return lax.dot_general(a, b, (((1,), (1,)), ((), ())), preferred_element_type=F32)


def _resident(shape):
    n = len(shape)
    return pl.BlockSpec(shape, lambda *_: (0,) * n, pipeline_mode=pl.Buffered(1))


def _params(n_grid):
    return pltpu.CompilerParams(dimension_semantics=("arbitrary",) * n_grid,
                                vmem_limit_bytes=VMEM_LIMIT_BYTES)


def _mod_kernel(c_ref, w_ref, b_ref, o_ref):
    c = c_ref[...]
    o_ref[...] = _dot(_silu(c).astype(BF16), w_ref[...].astype(BF16)) + b_ref[...]


def _mod_call(c_all, w_ada, b_ada):
    n, d = c_all.shape
    e = w_ada.shape[1]
    tn = e // 8
    return pl.pallas_call(
        _mod_kernel,
        grid=(e // tn,),
        in_specs=[pl.BlockSpec((n, d), lambda j: (0, 0)),
                  pl.BlockSpec((d, tn), lambda j: (0, j)),
                  pl.BlockSpec((1, tn), lambda j: (0, j))],
        out_specs=pl.BlockSpec((n, tn), lambda j: (0, j)),
        out_shape=jax.ShapeDtypeStruct((n, e), F32),
        compiler_params=_params(1),
        name="mod",
    )(c_all, w_ada, b_ada)


def _pre_front(x, shift1, scale1, n1w):
    ms = jnp.mean(x * x, axis=-1, keepdims=True)
    h = x * lax.rsqrt(ms + EPS) * n1w
    return (h * (1.0 + scale1) + shift1).astype(BF16)


def _qkv_finish(y, g, n_groups):
    s = _silu(y)
    n_qk = n_groups // 3
    if g < 2 * n_qk:
        s = s * lax.rsqrt(jnp.sum(s * s, axis=-1, keepdims=True) + EPS)
    if g < n_qk:
        s = s * (HEAD_DIM ** -0.5)
    return s


def _beta_decay(ba, alog, dtb):
    lane = lax.broadcasted_iota(jnp.int32, ba.shape, 1)
    beta = _sigmoid(ba)
    g = -jnp.exp(alog) * _softplus(ba + dtb)
    return jnp.where(lane < N_HEADS, beta, g)


def _pre_prompt_kernel(x_ref, mod_ref, n1w_ref, w_ref, wt_ref, cw_ref, alog_ref, dtb_ref, cum_ref,
                       glu_ref, qkv_ref, zg_ref, bg_ref, bgt_ref, tail_ref,
                       qkv_scr, h_scr, *, tm, tiles_per_seq, d, dc, dqkv):
    i = pl.program_id(0)
    h_scr[...] = _pre_front(x_ref[...], mod_ref[0, :, 0:d], mod_ref[0, :, d:2 * d], n1w_ref[...])

    u = _dot(h_scr[...], w_ref[:, 0:2 * dc])
    glu_ref[...] = u[:, :dc] * _sigmoid(u[:, dc:])

    @pl.when(i % tiles_per_seq == 0)
    def _():
        qkv_scr[:, 0:SUBLANES, :] = jnp.zeros((dqkv // LANES, SUBLANES, LANES), F32)

    o_qkv = 2 * dc
    o_z = o_qkv + dqkv
    step = 512
    n_chunks = dqkv // step
    n_taps = cw_ref.shape[0]

    n_groups = dqkv // LANES
    groups_per_chunk = step // LANES
    rows_per_phase = tm // CONV_PHASES

    def project(c):
        r = _dot(h_scr[...], w_ref[:, o_qkv + c * step:o_qkv + (c + 1) * step])
        for k in range(groups_per_chunk):
            g = c * groups_per_chunk + k
            qkv_scr[g, SUBLANES:SUBLANES + tm, :] = r[:, k * LANES:(k + 1) * LANES]
            tail_ref[:, g * LANES:(g + 1) * LANES] = qkv_scr[g, tm:tm + SUBLANES, :]

    def conv(c):
        for g in range(c * groups_per_chunk, (c + 1) * groups_per_chunk):
            cols = slice(g * LANES, (g + 1) * LANES)
            first = SUBLANES - (n_taps - 1)
            slabs = {start: qkv_scr[g, pl.ds(start, rows_per_phase, stride=CONV_PHASES), :]
                     for start in range(first, first + n_taps + CONV_PHASES - 1)}
            for p in range(CONV_PHASES):
                y = None
                for j in range(n_taps):
                    term = cw_ref[j:j + 1, cols] * slabs[first + j + p]
                    y = term if y is None else y + term
                qkv_ref[g, pl.ds(p, rows_per_phase, stride=CONV_PHASES), :] = _qkv_finish(y, g, n_groups)
            qkv_scr[g, 0:SUBLANES, :] = qkv_scr[g, tm:tm + SUBLANES, :]

    pieces = [(wsrc, o_w + half * step, k * d + half * step, act)
              for k, (wsrc, o_w, act) in enumerate([(w_ref, o_z, False), (wt_ref, 0, True), (wt_ref, d, True)])
              for half in range(d // step)]
    assert len(pieces) == n_chunks

    def side(c):
        wsrc, o_w, o_out, act = pieces[c]
        r = _dot(h_scr[...], wsrc[:, o_w:o_w + step])
        zg_ref[:, o_out:o_out + step] = (_sigmoid(r) if act else r).astype(BF16)

    project(0)
    for c in range(n_chunks):
        if c + 1 < n_chunks:
            project(c + 1)
        side(c)
        conv(c)

    ba = _dot(h_scr[...], wt_ref[:, 2 * d:2 * d + LANES])
    bg = _beta_decay(ba, alog_ref[...], dtb_ref[...])
    bg_hi = bg.astype(BF16)
    bg_lo = (bg - bg_hi.astype(F32)).astype(BF16)
    cum = _dot(cum_ref[...], bg_hi) + _dot(cum_ref[...], bg_lo)
    lane = lax.broadcasted_iota(jnp.int32, bg.shape, 1)
    bg = jnp.where(lane < N_HEADS, bg, cum)
    bg_ref[...] = bg
    bgt_ref[...] = bg.T


def _pre_prompt_call(x, mod_p, n1w, w_bf, w_tail, cw, alog_p, dtb_p, seq_len, dc):
    t, d = x.shape
    tm = PRE_TILE
    assert seq_len % tm == 0 and tm % CHUNK == 0 and tm % (CONV_PHASES * SUBLANES) == 0
    dqkv = cw.shape[1]
    w_main_shape = (d, 2 * dc + dqkv + d + LANES)
    nb = t // seq_len
    tiles_per_seq = seq_len // tm
    row = lax.broadcasted_iota(jnp.int32, (tm, tm), 0)
    col = lax.broadcasted_iota(jnp.int32, (tm, tm), 1)
    cum_mat = ((row // CHUNK == col // CHUNK) & (col <= row)).astype(BF16)

    tok = lambda w: pl.BlockSpec((tm, w), lambda i: (i, 0))
    kern = functools.partial(_pre_prompt_kernel, tm=tm, tiles_per_seq=tiles_per_seq, d=d, dc=dc, dqkv=dqkv)
    return pl.pallas_call(
        kern,
        grid=(t // tm,),
        in_specs=[tok(d),
                  pl.BlockSpec((1, 1, mod_p.shape[2]), lambda i: (i // tiles_per_seq, 0, 0)),
                  _resident(n1w.shape), _resident(w_main_shape), _resident(w_tail.shape), _resident(cw.shape),
                  _resident(alog_p.shape), _resident(dtb_p.shape), _resident(cum_mat.shape)],
        out_specs=[tok(dc), pl.BlockSpec((dqkv // LANES, tm, LANES), lambda i: (0, i, 0)), tok(3 * d), tok(LANES),
                   pl.BlockSpec((LANES, tm), lambda i: (0, i)),
                   pl.BlockSpec((SUBLANES, dqkv), lambda i: (i // tiles_per_seq, 0))],
        out_shape=[jax.ShapeDtypeStruct((t, dc), F32),
                   jax.ShapeDtypeStruct((dqkv // LANES, t, LANES), F32), jax.ShapeDtypeStruct((t, 3 * d), BF16),
                   jax.ShapeDtypeStruct((t, LANES), F32),
                   jax.ShapeDtypeStruct((LANES, t), F32),
                   jax.ShapeDtypeStruct((nb * SUBLANES, dqkv), F32)],
        scratch_shapes=[pltpu.VMEM((dqkv // LANES, tm + SUBLANES, LANES), F32), pltpu.VMEM((tm, d), BF16)],
        compiler_params=_params(1),
        name="pre_prompt",
    )(x, mod_p, n1w, w_bf, w_tail, cw, alog_p, dtb_p, cum_mat)


def _pre_sample_kernel(x_ref, mod_ref, n1w_ref, w_ref, wt_ref, cw_ref, alog_ref, dtb_ref,
                       sconf_ref, sqkv_ref, dww_ref, dwb_ref,
                       apre_ref, qkv_ref, zg_ref, bg_ref,
                       nconf_ref, nqkv_ref, *, d, dc, dqkv):
    hb = _pre_front(x_ref[...], mod_ref[:, 0:d], mod_ref[:, d:2 * d], n1w_ref[...])

    u = _dot(hb, w_ref[:, 0:2 * dc])
    glu = u[:, :dc] * _sigmoid(u[:, dc:])
    n_hist = sconf_ref.shape[0]
    acc = dww_ref[n_hist:n_hist + 1, :] * glu + dwb_ref[...]
    for j in range(n_hist):
        row = sconf_ref[j]
        acc = acc + dww_ref[j:j + 1, :] * row
        if j > 0:
            nconf_ref[j - 1] = row
    apre_ref[...] = acc
    nconf_ref[n_hist - 1] = glu

    o_qkv = 2 * dc
    n_taps = cw_ref.shape[0]
    for g in range(dqkv // LANES):
        cols = slice(g * LANES, (g + 1) * LANES)
        raw = _dot(hb, w_ref[:, o_qkv + g * LANES:o_qkv + (g + 1) * LANES])
        nqkv_ref[n_taps - 2, :, cols] = raw
        y = cw_ref[n_taps - 1:n_taps, cols] * raw
        for j in range(n_taps - 1):
            row = sqkv_ref[j, :, cols]
            y = y + cw_ref[j:j + 1, cols] * row
            if j > 0:
                nqkv_ref[j - 1, :, cols] = row
        qkv_ref[g] = _qkv_finish(y, g, dqkv // LANES)

    o_z = o_qkv + dqkv
    zg_ref[:, 0:d] = _dot(hb, w_ref[:, o_z:o_z + d]).astype(BF16)
    zg_ref[:, d:2 * d] = _sigmoid(_dot(hb, wt_ref[:, 0:d])).astype(BF16)
    zg_ref[:, 2 * d:3 * d] = _sigmoid(_dot(hb, wt_ref[:, d:2 * d])).astype(BF16)
    ba = _dot(hb, wt_ref[:, 2 * d:2 * d + LANES])
    bg_ref[...] = _beta_decay(ba, alog_ref[...], dtb_ref[...])


def _pre_sample_call(x, mod_s, n1w, w_bf, w_tail, cw, alog_p, dtb_p, sconf, sqkv, dww, dwb):
    n, d = x.shape
    dqkv = cw.shape[1]
    dc = dwb.shape[1]
    w_main_shape = (d, 2 * dc + dqkv + d + LANES)
    tm = 32
    assert n % tm == 0
    tok = lambda w: pl.BlockSpec((tm, w), lambda i: (i, 0))
    state = lambda a: pl.BlockSpec((a.shape[0], tm, a.shape[2]), lambda i: (0, i, 0))
    kern = functools.partial(_pre_sample_kernel, d=d, dc=dc, dqkv=dqkv)
    return pl.pallas_call(
        kern,
        grid=(n // tm,),
        in_specs=[tok(d), tok(mod_s.shape[1]),
                  _resident(n1w.shape), _resident(w_main_shape), _resident(w_tail.shape), _resident(cw.shape),
                  _resident(alog_p.shape), _resident(dtb_p.shape),
                  state(sconf), state(sqkv),
                  _resident(dww.shape), _resident(dwb.shape)],
        out_specs=[tok(dc), pl.BlockSpec((dqkv // LANES, tm, LANES), lambda i: (0, i, 0)), tok(3 * d), tok(LANES),
                   state(sconf), state(sqkv)],
        out_shape=[jax.ShapeDtypeStruct((n, dc), F32),
                   jax.ShapeDtypeStruct((dqkv // LANES, n, LANES), F32), jax.ShapeDtypeStruct((n, 3 * d), BF16),
                   jax.ShapeDtypeStruct((n, LANES), F32),
                   jax.ShapeDtypeStruct(sconf.shape, F32), jax.ShapeDtypeStruct(sqkv.shape, F32)],
        compiler_params=_params(1),
        name="pre_sample",
    )(x, mod_s, n1w, w_bf, w_tail, cw, alog_p, dtb_p, sconf, sqkv, dww, dwb)


def _half_block_mask(row, col, size):
    shift = size.bit_length() - 1
    half = size // 2
    return ((lax.shift_right_logical(row, shift) == lax.shift_right_logical(col, shift))
            & ((row & half) != 0) & ((col & half) == 0))


def _block_diag2(a, b):
    z = jnp.zeros(a.shape, a.dtype)
    return jnp.concatenate([jnp.concatenate([a, z], axis=1), jnp.concatenate([z, b], axis=1)], axis=0)


def _pair_dot(a, b):
    n = b[0].shape[1]
    out = _dot(jnp.concatenate(a, axis=1), _block_diag2(b[0], b[1]))
    return [out[:, :n], out[:, n:]]


def _gdn_prompt_kernel(q_ref, k_ref, v_ref, bg_ref, bgt_ref, o_ref, sfin_ref, s_scr, *, w, nw, n_steps):
    step = pl.program_id(1)

    @pl.when(step == 0)
    def _():
        s_scr[...] = jnp.zeros(s_scr.shape, F32)

    nc = w // CHUNK
    heads = range(N_HEADS)
    units = [(win, h) for win in range(nw) for h in heads]
    n_units = len(units)
    upairs = range(0, n_units, 2)
    row = lax.broadcasted_iota(jnp.int32, (w, w), 0)
    col = lax.broadcasted_iota(jnp.int32, (w, w), 1)
    shift = CHUNK.bit_length() - 1
    same = lax.shift_right_logical(row, shift) == lax.shift_right_logical(col, shift)
    m_incl = same & (col <= row)
    m_strict = same & (col < row)
    eye = (row == col).astype(F32)

    rs = [slice(win * w, (win + 1) * w) for win, _ in units]
    hs = [slice(h * HEAD_DIM, (h + 1) * HEAD_DIM) for _, h in units]
    qh = [q_ref[h, rs[u], :] for u, (_, h) in enumerate(units)]
    kh = [k_ref[h, rs[u], :] for u, (_, h) in enumerate(units)]
    vh = [v_ref[h, rs[u], :] for u, (_, h) in enumerate(units)]
    bcol, gcb, decay = [], [], []
    for u, (win, h) in enumerate(units):
        bcol.append(jnp.broadcast_to(bg_ref[rs[u], h:h + 1], (w, HEAD_DIM)))
        gcb.append(jnp.broadcast_to(bg_ref[rs[u], N_HEADS + h:N_HEADS + h + 1], (w, HEAD_DIM)))
        grow = bgt_ref[N_HEADS + h:N_HEADS + h + 1, rs[u]]
        decay.append(jnp.exp(jnp.where(m_incl, gcb[u] - grow, -1e30)))

    kb = [kh[u].astype(BF16) for u in range(n_units)]
    qb = [qh[u].astype(BF16) for u in range(n_units)]
    amat, att = [None] * n_units, [None] * n_units
    for p in upairs:
        lhs = jnp.concatenate([jnp.concatenate([kb[p], kb[p + 1]], axis=1),
                               jnp.concatenate([qb[p], qb[p + 1]], axis=1)], axis=0)
        r = _dot_nt(lhs, _block_diag2(kb[p], kb[p + 1]))
        for j in range(2):
            kk = r[:w, j * w:(j + 1) * w]
            amat[p + j] = jnp.where(m_strict, kk * bcol[p + j] * decay[p + j], 0.0)
            att[p + j] = (r[w:, j * w:(j + 1) * w] * decay[p + j]).astype(BF16)

    x = [eye - jnp.where(_half_block_mask(row, col, 2), amat[u], 0.0) for u in range(n_units)]
    size = 4
    while size <= CHUNK:
        mask = _half_block_mask(row, col, size)
        for p in upairs:
            blk = [jnp.where(mask, amat[p + j], 0.0).astype(BF16) for j in range(2)]
            xb = [x[p + j].astype(BF16) for j in range(2)]
            y = _pair_dot(blk, xb)
            upd = _pair_dot(xb, [y[j].astype(BF16) for j in range(2)])
            for j in range(2):
                x[p + j] = x[p + j] - upd[j]
        size *= 2

    eg = [jnp.exp(gcb[u]) for u in range(n_units)]
    kv = []
    for u in range(n_units):
        rhs = jnp.concatenate([kh[u] * (bcol[u] * eg[u]), vh[u] * bcol[u]], axis=1).astype(BF16)
        kv.append(_dot(x[u].astype(BF16), rhs).astype(BF16))
    qe, glast, ktt = [], [], []
    for u in range(n_units):
        qe.append(qh[u] * eg[u])
        gl = jnp.concatenate(
            [jnp.broadcast_to(gcb[u][(c + 1) * CHUNK - 1:(c + 1) * CHUNK, :], (CHUNK, HEAD_DIM)) for c in range(nc)],
            axis=0)
        glast.append(gl)
        ktt.append((kh[u] * jnp.exp(gl - gcb[u])).T.astype(BF16))
    wmat = [[None] * nc for _ in range(n_units)]
    bmat = [[None] * nc for _ in range(n_units)]
    qmat = [[None] * nc for _ in range(n_units)]
    omat = [[None] * nc for _ in range(n_units)]
    zeros = jnp.zeros((CHUNK, 2 * HEAD_DIM), BF16)
    for c in range(nc):
        cs = slice(c * CHUNK, (c + 1) * CHUNK)
        for u in range(n_units):
            parts = [zeros] * nc
            parts[c] = kv[u][cs]
            kv_c = jnp.concatenate(parts, axis=0)
            r = _dot(jnp.concatenate([ktt[u], att[u][cs]], axis=0), kv_c)
            wmat[u][c] = r[:HEAD_DIM, :HEAD_DIM]
            bmat[u][c] = r[:HEAD_DIM, HEAD_DIM:]
            qmat[u][c] = qe[u][cs] - r[HEAD_DIM:, :HEAD_DIM]
            omat[u][c] = r[HEAD_DIM:, HEAD_DIM:]

    s = [s_scr[h] for h in heads]
    for win in range(nw):
        for c in range(nc):
            cs = slice(win * w + c * CHUNK, win * w + (c + 1) * CHUNK)
            for hp in range(0, N_HEADS, 2):
                u = win * N_HEADS + hp
                lhs = [jnp.concatenate([wmat[u + j][c], qmat[u + j][c]], axis=0).astype(BF16) for j in range(2)]
                r = _pair_dot(lhs, [s[hp + j].astype(BF16) for j in range(2)])
                for j in range(2):
                    o_ref[cs, hs[u + j]] = r[j][HEAD_DIM:] + omat[u + j][c]
                    decay_c = jnp.exp(glast[u + j][c * CHUNK:c * CHUNK + 1, :])
                    s[hp + j] = s[hp + j] * decay_c - r[j][:HEAD_DIM] + bmat[u + j][c]
    for h in heads:
        s_scr[h] = s[h]

    @pl.when(step == n_steps - 1)
    def _():
        sfin_ref[0] = s_scr[...]


def _gdn_prompt_call(qkv, bg, bgt, nb, seq_len):
    t, d = qkv.shape[1], N_HEADS * HEAD_DIM
    w = GDN_WINDOW
    nw = GDN_WINDOWS_PER_STEP
    assert seq_len % (w * nw) == 0 and w % CHUNK == 0 and w == HEAD_DIM
    n_steps = seq_len // (w * nw)
    tok = pl.BlockSpec((w * nw, d), lambda b, i: (b * n_steps + i, 0))
    part = lambda k: pl.BlockSpec((N_HEADS, w * nw, HEAD_DIM), lambda b, i: (k, b * n_steps + i, 0))
    kern = functools.partial(_gdn_prompt_kernel, w=w, nw=nw, n_steps=n_steps)
    return pl.pallas_call(
        kern,
        grid=(nb, n_steps),
        in_specs=[part(0), part(1), part(2),
                  pl.BlockSpec((w * nw, LANES), lambda b, i: (b * n_steps + i, 0)),
                  pl.BlockSpec((LANES, w * nw), lambda b, i: (0, b * n_steps + i))],
        out_specs=[tok, pl.BlockSpec((1, N_HEADS, HEAD_DIM, HEAD_DIM), lambda b, i: (b, 0, 0, 0))],
        out_shape=[jax.ShapeDtypeStruct((t, d), F32),
                   jax.ShapeDtypeStruct((nb, N_HEADS, HEAD_DIM, HEAD_DIM), F32)],
        scratch_shapes=[pltpu.VMEM((N_HEADS, HEAD_DIM, HEAD_DIM), F32)],
        compiler_params=_params(2),
        name="gdn_prompt",
    )(qkv, qkv, qkv, bg, bgt)


def _gdn_sample_kernel(q_ref, k_ref, v_ref, bg_ref, s_ref, o_ref, snew_ref, *, bs):
    row = lax.broadcasted_iota(jnp.int32, (HEAD_DIM, HEAD_DIM), 0)
    col = lax.broadcasted_iota(jnp.int32, (HEAD_DIM, HEAD_DIM), 1)
    eye = row == col
    for h in range(N_HEADS):
        hs = slice(h * HEAD_DIM, (h + 1) * HEAD_DIM)
        qh = q_ref[h]
        kh = k_ref[h]
        vh = v_ref[h]
        beta = jnp.broadcast_to(bg_ref[:, h:h + 1], (bs, HEAD_DIM))
        eg = jnp.exp(jnp.broadcast_to(bg_ref[:, N_HEADS + h:N_HEADS + h + 1], (bs, HEAD_DIM)))
        qb = qh.astype(BF16)
        for i in range(bs):
            s = s_ref[i, h]
            kcol = jnp.sum(jnp.where(eye, jnp.broadcast_to(kh[i:i + 1, :], (HEAD_DIM, HEAD_DIM)), 0.0),
                           axis=-1, keepdims=True)
            ks = jnp.sum(kcol * s, axis=0, keepdims=True)
            egi = eg[i:i + 1, :]
            v_new = beta[i:i + 1, :] * (vh[i:i + 1, :] - egi * ks)
            s_new = s * egi + kcol * v_new
            snew_ref[i, h] = s_new
            o_ref[i:i + 1, hs] = _dot(qb, s_new.astype(BF16))[i:i + 1, :]


def _gdn_sample_call(qkv, bg, s0):
    n, d = qkv.shape[1], N_HEADS * HEAD_DIM
    bs = SAMPLE_SEQ_BLOCK
    assert n % bs == 0
    tok = lambda w: pl.BlockSpec((bs, w), lambda i: (i, 0))
    st = pl.BlockSpec((bs, N_HEADS, HEAD_DIM, HEAD_DIM), lambda i: (i, 0, 0, 0))
    return pl.pallas_call(
        functools.partial(_gdn_sample_kernel, bs=bs),
        grid=(n // bs,),
        in_specs=[pl.BlockSpec((N_HEADS, bs, HEAD_DIM), lambda i: (0, i, 0)),
                  pl.BlockSpec((N_HEADS, bs, HEAD_DIM), lambda i: (1, i, 0)),
                  pl.BlockSpec((N_HEADS, bs, HEAD_DIM), lambda i: (2, i, 0)), tok(LANES), st],
        out_specs=[tok(d), st],
        out_shape=[jax.ShapeDtypeStruct((n, d), F32), jax.ShapeDtypeStruct(s0.shape, F32)],
        compiler_params=_params(1),
        name="gdn_sample",
    )(qkv, qkv, qkv, bg, s0)


def _rms(x, w):
    return x * lax.rsqrt(jnp.mean(x * x, axis=-1, keepdims=True) + EPS) * w


def _ln_swish(a, lnw_ref, lnb_ref):
    mu = jnp.mean(a, axis=-1, keepdims=True)
    ac = a - mu
    var = jnp.mean(ac * ac, axis=-1, keepdims=True)
    return _silu(ac * lax.rsqrt(var + EPS) * lnw_ref[...] + lnb_ref[...]).astype(BF16)


def _conv_ln_swish_jobs(ext_scr, cv_scr, a_scr, dww_ref, dwb_ref, lnw_ref, lnb_ref, *, tm, dc):
    n_taps = dww_ref.shape[0]
    rows_per_phase = tm // CONV_PHASES
    first = CONV_HALO - (n_taps - 1)

    def taps(g):
        cols = slice(g * LANES, (g + 1) * LANES)
        acc = [None] * CONV_PHASES
        for start in range(first, first + n_taps + CONV_PHASES - 1):
            src = ext_scr[g, pl.ds(start, rows_per_phase, stride=CONV_PHASES), :]
            for p in range(CONV_PHASES):
                j = start - first - p
                if 0 <= j < n_taps:
                    term = dww_ref[j:j + 1, cols] * src
                    acc[p] = term if acc[p] is None else acc[p] + term
        for p in range(CONV_PHASES):
            cv_scr[g, pl.ds(p, rows_per_phase, stride=CONV_PHASES), :] = acc[p] + dwb_ref[:, cols]

    def finish():
        a = jnp.concatenate([cv_scr[g] for g in range(dc // LANES)], axis=1)
        a_scr[...] = _ln_swish(a, lnw_ref, lnb_ref)

    return [functools.partial(taps, g) for g in range(dc // LANES)] + [finish]


def _fill_conv_history(ext_scr, hist, new, *, tm, dc):
    for g in range(dc // LANES):
        cols = slice(g * LANES, (g + 1) * LANES)
        ext_scr[g, 0:CONV_HALO, :] = hist[:, cols]
        ext_scr[g, CONV_HALO:CONV_HALO + tm, :] = new[:, cols]


def _post_tail(a, x_ref, mod, o_ref, zg_ref, gnw_ref, wca_ref, wgo_ref, wo_ref, n2w_ref,
               w1_ref, w2_ref, fnw_ref, y_ref, *, dff, side_jobs=()):
    side = list(side_jobs)

    def run_side():
        if side:
            side.pop(0)()

    d = x_ref.shape[1]
    y_a = _dot(a, wca_ref[:, 0:d])
    run_side()

    gated = []
    for h in range(N_HEADS):
        hs = slice(h * HEAD_DIM, (h + 1) * HEAD_DIM)
        gated.append(_rms(o_ref[:, hs], gnw_ref[...]) * _silu(zg_ref[:, hs].astype(F32)))
    y_b = _dot(jnp.concatenate(gated, axis=1).astype(BF16), wgo_ref[:, 0:d])
    run_side()

    merged = zg_ref[:, d:2 * d].astype(F32) * y_a + zg_ref[:, 2 * d:3 * d].astype(F32) * y_b
    x1 = x_ref[...] + mod(2) * _dot(merged.astype(BF16), wo_ref[:, 0:d])
    run_side()

    h2 = (_rms(x1, n2w_ref[...]) * (1.0 + mod(4)) + mod(3)).astype(BF16)
    step = 1024
    ff = None
    for c in range(dff // step):
        f = jnp.maximum(_dot(h2, w1_ref[:, c * step:(c + 1) * step]), 0.0)
        part = _dot((f * f).astype(BF16), w2_ref[c * step:(c + 1) * step, 0:d])
        ff = part if ff is None else ff + part
        run_side()
    while side:
        run_side()
    x2 = x1 + mod(5) * ff
    y_ref[...] = _rms(x2, fnw_ref[...])


def _post_prompt_kernel(x_ref, mod_ref, glu_ref, glun_ref, dww_ref, dwb_ref, lnw_ref, lnb_ref,
                        o_ref, zg_ref, gnw_ref, wca_ref, wgo_ref, wo_ref, n2w_ref,
                        w1_ref, w2_ref, fnw_ref, y_ref, ext_scr, cv_scr, a_scr,
                        *, tm, tiles_per_seq, d, dc, dff):
    i = pl.program_id(0)
    conv_jobs = functools.partial(_conv_ln_swish_jobs, ext_scr, cv_scr, a_scr, dww_ref, dwb_ref,
                                  lnw_ref, lnb_ref, tm=tm, dc=dc)

    @pl.when(i == 0)
    def _():
        _fill_conv_history(ext_scr, jnp.zeros((CONV_HALO, dc), F32), glu_ref[...], tm=tm, dc=dc)
        for job in conv_jobs():
            job()

    a = a_scr[...]
    hist = jnp.where((i + 1) % tiles_per_seq == 0, 0.0, glu_ref[tm - CONV_HALO:tm, :])
    _fill_conv_history(ext_scr, hist, glun_ref[...], tm=tm, dc=dc)

    mod = lambda k: mod_ref[0, :, k * d:(k + 1) * d]
    _post_tail(a, x_ref, mod, o_ref, zg_ref, gnw_ref, wca_ref, wgo_ref, wo_ref, n2w_ref,
               w1_ref, w2_ref, fnw_ref, y_ref, dff=dff, side_jobs=conv_jobs())


def _post_sample_kernel(x_ref, mod_ref, apre_ref, lnw_ref, lnb_ref,
                        o_ref, zg_ref, gnw_ref, wca_ref, wgo_ref, wo_ref, n2w_ref,
                        w1_ref, w2_ref, fnw_ref, y_ref, *, d, dff):
    a = _ln_swish(apre_ref[...], lnw_ref, lnb_ref)
    mod = lambda k: mod_ref[:, k * d:(k + 1) * d]
    _post_tail(a, x_ref, mod, o_ref, zg_ref, gnw_ref, wca_ref, wgo_ref, wo_ref, n2w_ref,
               w1_ref, w2_ref, fnw_ref, y_ref, dff=dff)


def _post_call(x, mod, conv_in, lnw, lnb, o, zg, gnw, wca, wgo, wo, n2w, w1, w2, fnw, *, seq_len):
    t, d = x.shape
    dc = lnw.shape[1]
    dff = w2.shape[0]
    prompt = len(conv_in) == 3
    tm = min(POST_TILE, t)
    assert t % tm == 0
    n_tiles = t // tm
    tok = lambda w: pl.BlockSpec((tm, w), lambda i: (i, 0))
    common_args = [lnw, lnb, o, zg, gnw, wca, wgo, wo, n2w, w1, w2, fnw]
    common_specs = [_resident(lnw.shape), _resident(lnb.shape), tok(d), tok(3 * d), _resident(gnw.shape),
                    _resident(wca.shape), _resident(wgo.shape), _resident(wo.shape), _resident(n2w.shape),
                    _resident(w1.shape), _resident(w2.shape), _resident(fnw.shape)]
    if prompt:
        glu, dww, dwb = conv_in
        assert seq_len % tm == 0 and tm >= CONV_HALO >= dww.shape[0] - 1
        tiles_per_seq = seq_len // tm
        args = [x, mod, glu, glu, dww, dwb] + common_args
        specs = [tok(d), pl.BlockSpec((1, 1, mod.shape[2]), lambda i: (i // tiles_per_seq, 0, 0)),
                 tok(dc), pl.BlockSpec((tm, dc), lambda i: (jnp.minimum(i + 1, n_tiles - 1), 0)),
                 _resident(dww.shape), _resident(dwb.shape)] + common_specs
        scratch = [pltpu.VMEM((dc // LANES, tm + CONV_HALO, LANES), F32),
                   pltpu.VMEM((dc // LANES, tm, LANES), F32),
                   pltpu.VMEM((tm, dc), BF16)]
        kern = functools.partial(_post_prompt_kernel, tm=tm, tiles_per_seq=tiles_per_seq, d=d, dc=dc, dff=dff)
        name = "post_prompt"
    else:
        args = [x, mod, conv_in[0]] + common_args
        specs = [tok(d), tok(mod.shape[1]), tok(dc)] + common_specs
        scratch = []
        kern = functools.partial(_post_sample_kernel, d=d, dff=dff)
        name = "post_sample"
    return pl.pallas_call(
        kern,
        grid=(n_tiles,),
        in_specs=specs,
        out_specs=tok(d),
        out_shape=jax.ShapeDtypeStruct((t, d), F32),
        scratch_shapes=scratch,
        compiler_params=_params(1),
        name=name,
    )(*args)


def _split_w_in(w_in, d, dc, dqkv):
    w_bf = w_in.astype(BF16)
    o1 = 2 * dc + dqkv + d
    ba = w_bf[:, o1:o1 + 2 * N_HEADS]
    gates = w_bf[:, o1 + 2 * N_HEADS:]
    pad = jnp.zeros((w_bf.shape[0], LANES - 2 * N_HEADS), BF16)
    return w_bf, jnp.concatenate([gates, ba, pad], axis=1)


def _odd_lane_tiles(w):
    w = w.astype(BF16)
    if (w.shape[1] // LANES) % 2 == 0:
        w = jnp.pad(w, ((0, 0), (0, LANES)))
    return w


def _lane_pad(vec, offset):
    out = jnp.zeros((1, LANES), F32)
    return out.at[0, offset:offset + vec.shape[0]].set(vec.astype(F32))


def kernel(x_prompt, x_sample, c_prompt, c_sample, state_conf_conv, state_qkv_conv, state_delta, w_ada, b_ada, norm1_w, w_in, conf_dw_w, conf_dw_b, conf_ln_w, conf_ln_b, w_conf_out, gdn_conv_w, a_log, dt_bias, gdn_norm_w, w_gdn_out, w_o, norm2_w, w_ff1, w_ff2, final_norm_w):
    nb, seq_len, d = x_prompt.shape
    ns = x_sample.shape[0]
    depth = w_ada.shape[0]
    dc = conf_dw_w.shape[2]
    dqkv = gdn_conv_w.shape[2]
    n_hist = conf_dw_w.shape[1] - 1
    n_taps = gdn_conv_w.shape[1]
    assert x_sample.shape[1] == 1

    xp = x_prompt.reshape(nb * seq_len, d)
    xs = x_sample.reshape(ns, d)
    c_all = jnp.concatenate([c_sample, c_prompt], axis=0)
    row2 = lambda v: v.reshape(1, -1).astype(F32)

    conf_p, qkv_p, delta_p, conf_s, qkv_s, delta_s = [], [], [], [], [], []
    for l in range(depth):
        mod = _mod_call(c_all, w_ada[l].astype(F32), row2(b_ada[l]))
        mod_p = mod[ns:].reshape(nb, 1, N_MOD * d)
        mod_s = mod
        w_bf, w_tail = _split_w_in(w_in[l], d, dc, dqkv)
        n1w, n2w, fnw = row2(norm1_w[l]), row2(norm2_w[l]), row2(final_norm_w)
        alog_p = _lane_pad(a_log[l], N_HEADS)
        dtb_p = _lane_pad(dt_bias[l], N_HEADS)
        cw = gdn_conv_w[l].astype(F32)
        dww, dwb = conf_dw_w[l].astype(F32), row2(conf_dw_b[l])
        lnw, lnb = row2(conf_ln_w[l]), row2(conf_ln_b[l])
        gnw = row2(gdn_norm_w[l])
        wca, wgo, wo = _odd_lane_tiles(w_conf_out[l]), _odd_lane_tiles(w_gdn_out[l]), _odd_lane_tiles(w_o[l])
        w1, w2 = _odd_lane_tiles(w_ff1[l]), _odd_lane_tiles(w_ff2[l])
        assert depth == 1

        glu, qkv, zg, bg, bgt, tail = _pre_prompt_call(
            xp, mod_p, n1w, w_bf, w_tail, cw, alog_p, dtb_p, seq_len, dc)
        o, s_fin = _gdn_prompt_call(qkv, bg, bgt, nb, seq_len)
        xp = _post_call(xp, mod_p, (glu, dww, dwb), lnw, lnb, o, zg, gnw, wca, wgo, wo, n2w, w1, w2, fnw,
                        seq_len=seq_len)
        conf_p.append(glu.reshape(nb, seq_len, dc)[:, seq_len - n_hist:])
        qkv_p.append(tail.reshape(nb, SUBLANES, dqkv)[:, SUBLANES - (n_taps - 1):])
        delta_p.append(s_fin)

        sconf = jnp.transpose(state_conf_conv[l].astype(F32), (1, 0, 2))
        sqkv = jnp.transpose(state_qkv_conv[l].astype(F32), (1, 0, 2))
        apre, qkv, zg, bg, nconf, nqkv = _pre_sample_call(
            xs, mod_s, n1w, w_bf, w_tail, cw, alog_p, dtb_p, sconf, sqkv, dww, dwb)
        o, s_new = _gdn_sample_call(qkv, bg, state_delta[l].astype(F32))
        xs = _post_call(xs, mod_s, (apre,), lnw, lnb, o, zg, gnw, wca, wgo, wo, n2w, w1, w2, fnw,
                        seq_len=1)
        conf_s.append(jnp.transpose(nconf, (1, 0, 2)))
        qkv_s.append(jnp.transpose(nqkv, (1, 0, 2)))
        delta_s.append(s_new)

    return (xp.reshape(nb, seq_len, d), xs.reshape(ns, 1, d),
            jnp.stack(conf_p), jnp.stack(qkv_p), jnp.stack(delta_p),
            jnp.stack(conf_s), jnp.stack(qkv_s), jnp.stack(delta_s))
```

```python
import functools

import jax
import jax.numpy as jnp
from jax import lax
from jax.experimental import pallas as pl
from jax.experimental.pallas import tpu as pltpu

F32 = jnp.float32
BF16 = jnp.bfloat16

EPS = 1e-6
N_MOD = 6
N_HEADS = 8
HEAD_DIM = 128
CHUNK = 64
LANES = 128
SUBLANES = 8
VMEM_LIMIT_BYTES = 56 * 1024 * 1024

PRE_TILE = 256
POST_TILE = 256
GDN_WINDOW = 128
GDN_WINDOWS_PER_STEP = 4
CONV_HALO = 32
SAMPLE_SEQ_BLOCK = 16
CONV_PHASES = 4


NEG_LOG2_E = -1.4426950408889634


def _sigmoid(x):
    return 1.0 / (1.0 + jnp.exp2(x * NEG_LOG2_E))


def _silu(x):
    return x * _sigmoid(x)


def _softplus(x):
    return jnp.maximum(x, 0.0) + jnp.log(1.0 + jnp.exp(-jnp.abs(x)))


def _dot(a, b):
    return jnp.dot(a, b, preferred_element_type=F32)


def _dot_nt(a, b):
    return lax.dot_general(a, b, (((1,), (1,)), ((), ())), preferred_element_type=F32)


def _resident(shape):
    n = len(shape)
    return pl.BlockSpec(shape, lambda *_: (0,) * n, pipeline_mode=pl.Buffered(1))


def _params(n_grid):
    return pltpu.CompilerParams(dimension_semantics=("arbitrary",) * n_grid,
                                vmem_limit_bytes=VMEM_LIMIT_BYTES)


def _mod_kernel(c_ref, w_ref, b_ref, o_ref):
    c = c_ref[...]
    o_ref[...] = _dot(_silu(c).astype(BF16), w_ref[...].astype(BF16)) + b_ref[...]


def _mod_call(c_all, w_ada, b_ada):
    n, d = c_all.shape
    e = w_ada.shape[1]
    tn = e // 8
    return pl.pallas_call(
        _mod_kernel,
        grid=(e // tn,),
        in_specs=[pl.BlockSpec((n, d), lambda j: (0, 0)),
                  pl.BlockSpec((d, tn), lambda j: (0, j)),
                  pl.BlockSpec((1, tn), lambda j: (0, j))],
        out_specs=pl.BlockSpec((n, tn), lambda j: (0, j)),
        out_shape=jax.ShapeDtypeStruct((n, e), F32),
        compiler_params=_params(1),
        name="mod",
    )(c_all, w_ada, b_ada)


def _pre_front(x, shift1, scale1, n1w):
    ms = jnp.mean(x * x, axis=-1, keepdims=True)
    h = x * lax.rsqrt(ms + EPS) * n1w
    return (h * (1.0 + scale1) + shift1).astype(BF16)


def _qkv_finish(y, g, n_groups):
    s = _silu(y)
    n_qk = n_groups // 3
    if g < 2 * n_qk:
        s = s * lax.rsqrt(jnp.sum(s * s, axis=-1, keepdims=True) + EPS)
    if g < n_qk:
        s = s * (HEAD_DIM ** -0.5)
    return s


def _beta_decay(ba, alog, dtb):
    lane = lax.broadcasted_iota(jnp.int32, ba.shape, 1)
    beta = _sigmoid(ba)
    g = -jnp.exp(alog) * _softplus(ba + dtb)
    return jnp.where(lane < N_HEADS, beta, g)


def _pre_prompt_kernel(x_ref, mod_ref, n1w_ref, w_ref, wt_ref, cw_ref, alog_ref, dtb_ref, cum_ref,
                       glu_ref, qkv_ref, zg_ref, bg_ref, bgt_ref, tail_ref,
                       qkv_scr, h_scr, *, tm, tiles_per_seq, d, dc, dqkv):
    i = pl.program_id(0)
    h_scr[...] = _pre_front(x_ref[...], mod_ref[0, :, 0:d], mod_ref[0, :, d:2 * d], n1w_ref[...])

    u = _dot(h_scr[...], w_ref[:, 0:2 * dc])
    glu_ref[...] = u[:, :dc] * _sigmoid(u[:, dc:])

    @pl.when(i % tiles_per_seq == 0)
    def _():
        qkv_scr[:, 0:SUBLANES, :] = jnp.zeros((dqkv // LANES, SUBLANES, LANES), F32)

    o_qkv = 2 * dc
    o_z = o_qkv + dqkv
    step = 512
    n_chunks = dqkv // step
    n_taps = cw_ref.shape[0]

    n_groups = dqkv // LANES
    groups_per_chunk = step // LANES
    rows_per_phase = tm // CONV_PHASES

    def project(c):
        r = _dot(h_scr[...], w_ref[:, o_qkv + c * step:o_qkv + (c + 1) * step])
        for k in range(groups_per_chunk):
            g = c * groups_per_chunk + k
            qkv_scr[g, SUBLANES:SUBLANES + tm, :] = r[:, k * LANES:(k + 1) * LANES]
            tail_ref[:, g * LANES:(g + 1) * LANES] = qkv_scr[g, tm:tm + SUBLANES, :]

    def conv(c):
        for g in range(c * groups_per_chunk, (c + 1) * groups_per_chunk):
            cols = slice(g * LANES, (g + 1) * LANES)
            first = SUBLANES - (n_taps - 1)
            slabs = {start: qkv_scr[g, pl.ds(start, rows_per_phase, stride=CONV_PHASES), :]
                     for start in range(first, first + n_taps + CONV_PHASES - 1)}
            for p in range(CONV_PHASES):
                y = None
                for j in range(n_taps):
                    term = cw_ref[j:j + 1, cols] * slabs[first + j + p]
                    y = term if y is None else y + term
                qkv_ref[g, pl.ds(p, rows_per_phase, stride=CONV_PHASES), :] = _qkv_finish(y, g, n_groups)
            qkv_scr[g, 0:SUBLANES, :] = qkv_scr[g, tm:tm + SUBLANES, :]

    pieces = [(wsrc, o_w + half * step, k * d + half * step, act)
              for k, (wsrc, o_w, act) in enumerate([(w_ref, o_z, False), (wt_ref, 0, True), (wt_ref, d, True)])
              for half in range(d // step)]
    assert len(pieces) == n_chunks

    def side(c):
        wsrc, o_w, o_out, act = pieces[c]
        r = _dot(h_scr[...], wsrc[:, o_w:o_w + step])
        zg_ref[:, o_out:o_out + step] = (_sigmoid(r) if act else r).astype(BF16)

    project(0)
    for c in range(n_chunks):
        if c + 1 < n_chunks:
            project(c + 1)
        side(c)
        conv(c)

    ba = _dot(h_scr[...], wt_ref[:, 2 * d:2 * d + LANES])
    bg = _beta_decay(ba, alog_ref[...], dtb_ref[...])
    bg_hi = bg.astype(BF16)
    bg_lo = (bg - bg_hi.astype(F32)).astype(BF16)
    cum = _dot(cum_ref[...], bg_hi) + _dot(cum_ref[...], bg_lo)
    lane = lax.broadcasted_iota(jnp.int32, bg.shape, 1)
    bg = jnp.where(lane < N_HEADS, bg, cum)
    bg_ref[...] = bg
    bgt_ref[...] = bg.T


def _pre_prompt_call(x, mod_p, n1w, w_bf, w_tail, cw, alog_p, dtb_p, seq_len, dc):
    t, d = x.shape
    tm = PRE_TILE
    assert seq_len % tm == 0 and tm % CHUNK == 0 and tm % (CONV_PHASES * SUBLANES) == 0
    dqkv = cw.shape[1]
    w_main_shape = (d, 2 * dc + dqkv + d + LANES)
    nb = t // seq_len
    tiles_per_seq = seq_len // tm
    row = lax.broadcasted_iota(jnp.int32, (tm, tm), 0)
    col = lax.broadcasted_iota(jnp.int32, (tm, tm), 1)
    cum_mat = ((row // CHUNK == col // CHUNK) & (col <= row)).astype(BF16)

    tok = lambda w: pl.BlockSpec((tm, w), lambda i: (i, 0))
    kern = functools.partial(_pre_prompt_kernel, tm=tm, tiles_per_seq=tiles_per_seq, d=d, dc=dc, dqkv=dqkv)
    return pl.pallas_call(
        kern,
        grid=(t // tm,),
        in_specs=[tok(d),
                  pl.BlockSpec((1, 1, mod_p.shape[2]), lambda i: (i // tiles_per_seq, 0, 0)),
                  _resident(n1w.shape), _resident(w_main_shape), _resident(w_tail.shape), _resident(cw.shape),
                  _resident(alog_p.shape), _resident(dtb_p.shape), _resident(cum_mat.shape)],
        out_specs=[tok(dc), pl.BlockSpec((dqkv // LANES, tm, LANES), lambda i: (0, i, 0)), tok(3 * d), tok(LANES),
                   pl.BlockSpec((LANES, tm), lambda i: (0, i)),
                   pl.BlockSpec((SUBLANES, dqkv), lambda i: (i // tiles_per_seq, 0))],
        out_shape=[jax.ShapeDtypeStruct((t, dc), F32),
                   jax.ShapeDtypeStruct((dqkv // LANES, t, LANES), F32), jax.ShapeDtypeStruct((t, 3 * d), BF16),
                   jax.ShapeDtypeStruct((t, LANES), F32),
                   jax.ShapeDtypeStruct((LANES, t), F32),
                   jax.ShapeDtypeStruct((nb * SUBLANES, dqkv), F32)],
        scratch_shapes=[pltpu.VMEM((dqkv // LANES, tm + SUBLANES, LANES), F32), pltpu.VMEM((tm, d), BF16)],
        compiler_params=_params(1),
        name="pre_prompt",
    )(x, mod_p, n1w, w_bf, w_tail, cw, alog_p, dtb_p, cum_mat)


def _pre_sample_kernel(x_ref, mod_ref, n1w_ref, w_ref, wt_ref, cw_ref, alog_ref, dtb_ref,
                       sconf_ref, sqkv_ref, dww_ref, dwb_ref,
                       apre_ref, qkv_ref, zg_ref, bg_ref,
                       nconf_ref, nqkv_ref, *, d, dc, dqkv):
    hb = _pre_front(x_ref[...], mod_ref[:, 0:d], mod_ref[:, d:2 * d], n1w_ref[...])

    u = _dot(hb, w_ref[:, 0:2 * dc])
    glu = u[:, :dc] * _sigmoid(u[:, dc:])
    n_hist = sconf_ref.shape[0]
    acc = dww_ref[n_hist:n_hist + 1, :] * glu + dwb_ref[...]
    for j in range(n_hist):
        row = sconf_ref[j]
        acc = acc + dww_ref[j:j + 1, :] * row
        if j > 0:
            nconf_ref[j - 1] = row
    apre_ref[...] = acc
    nconf_ref[n_hist - 1] = glu

    o_qkv = 2 * dc
    n_taps = cw_ref.shape[0]
    for g in range(dqkv // LANES):
        cols = slice(g * LANES, (g + 1) * LANES)
        raw = _dot(hb, w_ref[:, o_qkv + g * LANES:o_qkv + (g + 1) * LANES])
        nqkv_ref[n_taps - 2, :, cols] = raw
        y = cw_ref[n_taps - 1:n_taps, cols] * raw
        for j in range(n_taps - 1):
            row = sqkv_ref[j, :, cols]
            y = y + cw_ref[j:j + 1, cols] * row
            if j > 0:
                nqkv_ref[j - 1, :, cols] = row
        qkv_ref[g] = _qkv_finish(y, g, dqkv // LANES)

    o_z = o_qkv + dqkv
    zg_ref[:, 0:d] = _dot(hb, w_ref[:, o_z:o_z + d]).astype(BF16)
    zg_ref[:, d:2 * d] = _sigmoid(_dot(hb, wt_ref[:, 0:d])).astype(BF16)
    zg_ref[:, 2 * d:3 * d] = _sigmoid(_dot(hb, wt_ref[:, d:2 * d])).astype(BF16)
    ba = _dot(hb, wt_ref[:, 2 * d:2 * d + LANES])
    bg_ref[...] = _beta_decay(ba, alog_ref[...], dtb_ref[...])


def _pre_sample_call(x, mod_s, n1w, w_bf, w_tail, cw, alog_p, dtb_p, sconf, sqkv, dww, dwb):
    n, d = x.shape
    dqkv = cw.shape[1]
    dc = dwb.shape[1]
    w_main_shape = (d, 2 * dc + dqkv + d + LANES)
    tm = 32
    assert n % tm == 0
    tok = lambda w: pl.BlockSpec((tm, w), lambda i: (i, 0))
    state = lambda a: pl.BlockSpec((a.shape[0], tm, a.shape[2]), lambda i: (0, i, 0))
    kern = functools.partial(_pre_sample_kernel, d=d, dc=dc, dqkv=dqkv)
    return pl.pallas_call(
        kern,
        grid=(n // tm,),
        in_specs=[tok(d), tok(mod_s.shape[1]),
                  _resident(n1w.shape), _resident(w_main_shape), _resident(w_tail.shape), _resident(cw.shape),
                  _resident(alog_p.shape), _resident(dtb_p.shape),
                  state(sconf), state(sqkv),
                  _resident(dww.shape), _resident(dwb.shape)],
        out_specs=[tok(dc), pl.BlockSpec((dqkv // LANES, tm, LANES), lambda i: (0, i, 0)), tok(3 * d), tok(LANES),
                   state(sconf), state(sqkv)],
        out_shape=[jax.ShapeDtypeStruct((n, dc), F32),
                   jax.ShapeDtypeStruct((dqkv // LANES, n, LANES), F32), jax.ShapeDtypeStruct((n, 3 * d), BF16),
                   jax.ShapeDtypeStruct((n, LANES), F32),
                   jax.ShapeDtypeStruct(sconf.shape, F32), jax.ShapeDtypeStruct(sqkv.shape, F32)],
        compiler_params=_params(1),
        name="pre_sample",
    )(x, mod_s, n1w, w_bf, w_tail, cw, alog_p, dtb_p, sconf, sqkv, dww, dwb)


def _half_block_mask(row, col, size):
    shift = size.bit_length() - 1
    half = size // 2
    return ((lax.shift_right_logical(row, shift) == lax.shift_right_logical(col, shift))
            & ((row & half) != 0) & ((col & half) == 0))


def _block_diag2(a, b):
    z = jnp.zeros(a.shape, a.dtype)
    return jnp.concatenate([jnp.concatenate([a, z], axis=1), jnp.concatenate([z, b], axis=1)], axis=0)


def _pair_dot(a, b):
    n = b[0].shape[1]
    out = _dot(jnp.concatenate(a, axis=1), _block_diag2(b[0], b[1]))
    return [out[:, :n], out[:, n:]]


def _gdn_prompt_kernel(q_ref, k_ref, v_ref, bg_ref, bgt_ref, o_ref, sfin_ref, s_scr, *, w, nw, n_steps):
    step = pl.program_id(1)

    @pl.when(step == 0)
    def _():
        s_scr[...] = jnp.zeros(s_scr.shape, F32)

    nc = w // CHUNK
    heads = range(N_HEADS)
    units = [(win, h) for win in range(nw) for h in heads]
    n_units = len(units)
    upairs = range(0, n_units, 2)
    row = lax.broadcasted_iota(jnp.int32, (w, w), 0)
    col = lax.broadcasted_iota(jnp.int32, (w, w), 1)
    shift = CHUNK.bit_length() - 1
    same = lax.shift_right_logical(row, shift) == lax.shift_right_logical(col, shift)
    m_incl = same & (col <= row)
    m_strict = same & (col < row)
    eye = (row == col).astype(F32)

    rs = [slice(win * w, (win + 1) * w) for win, _ in units]
    hs = [slice(h * HEAD_DIM, (h + 1) * HEAD_DIM) for _, h in units]
    qh = [q_ref[h, rs[u], :] for u, (_, h) in enumerate(units)]
    kh = [k_ref[h, rs[u], :] for u, (_, h) in enumerate(units)]
    vh = [v_ref[h, rs[u], :] for u, (_, h) in enumerate(units)]
    bcol, gcb, decay = [], [], []
    for u, (win, h) in enumerate(units):
        bcol.append(jnp.broadcast_to(bg_ref[rs[u], h:h + 1], (w, HEAD_DIM)))
        gcb.append(jnp.broadcast_to(bg_ref[rs[u], N_HEADS + h:N_HEADS + h + 1], (w, HEAD_DIM)))
        grow = bgt_ref[N_HEADS + h:N_HEADS + h + 1, rs[u]]
        decay.append(jnp.exp(jnp.where(m_incl, gcb[u] - grow, -1e30)))

    kb = [kh[u].astype(BF16) for u in range(n_units)]
    qb = [qh[u].astype(BF16) for u in range(n_units)]
    amat, att = [None] * n_units, [None] * n_units
    for p in upairs:
        lhs = jnp.concatenate([jnp.concatenate([kb[p], kb[p + 1]], axis=1),
                               jnp.concatenate([qb[p], qb[p + 1]], axis=1)], axis=0)
        r = _dot_nt(lhs, _block_diag2(kb[p], kb[p + 1]))
        for j in range(2):
            kk = r[:w, j * w:(j + 1) * w]
            amat[p + j] = jnp.where(m_strict, kk * bcol[p + j] * decay[p + j], 0.0)
            att[p + j] = (r[w:, j * w:(j + 1) * w] * decay[p + j]).astype(BF16)

    x = [eye - jnp.where(_half_block_mask(row, col, 2), amat[u], 0.0) for u in range(n_units)]
    size = 4
    while size <= CHUNK:
        mask = _half_block_mask(row, col, size)
        for p in upairs:
            blk = [jnp.where(mask, amat[p + j], 0.0).astype(BF16) for j in range(2)]
            xb = [x[p + j].astype(BF16) for j in range(2)]
            y = _pair_dot(blk, xb)
            upd = _pair_dot(xb, [y[j].astype(BF16) for j in range(2)])
            for j in range(2):
                x[p + j] = x[p + j] - upd[j]
        size *= 2

    eg = [jnp.exp(gcb[u]) for u in range(n_units)]
    kv = []
    for u in range(n_units):
        rhs = jnp.concatenate([kh[u] * (bcol[u] * eg[u]), vh[u] * bcol[u]], axis=1).astype(BF16)
        kv.append(_dot(x[u].astype(BF16), rhs).astype(BF16))
    qe, glast, ktt = [], [], []
    for u in range(n_units):
        qe.append(qh[u] * eg[u])
        gl = jnp.concatenate(
            [jnp.broadcast_to(gcb[u][(c + 1) * CHUNK - 1:(c + 1) * CHUNK, :], (CHUNK, HEAD_DIM)) for c in range(nc)],
            axis=0)
        glast.append(gl)
        ktt.append((kh[u] * jnp.exp(gl - gcb[u])).T.astype(BF16))
    wmat = [[None] * nc for _ in range(n_units)]
    bmat = [[None] * nc for _ in range(n_units)]
    qmat = [[None] * nc for _ in range(n_units)]
    omat = [[None] * nc for _ in range(n_units)]
    zeros = jnp.zeros((CHUNK, 2 * HEAD_DIM), BF16)
    for c in range(nc):
        cs = slice(c * CHUNK, (c + 1) * CHUNK)
        for u in range(n_units):
            parts = [zeros] * nc
            parts[c] = kv[u][cs]
            kv_c = jnp.concatenate(parts, axis=0)
            r = _dot(jnp.concatenate([ktt[u], att[u][cs]], axis=0), kv_c)
            wmat[u][c] = r[:HEAD_DIM, :HEAD_DIM]
            bmat[u][c] = r[:HEAD_DIM, HEAD_DIM:]
            qmat[u][c] = qe[u][cs] - r[HEAD_DIM:, :HEAD_DIM]
            omat[u][c] = r[HEAD_DIM:, HEAD_DIM:]

    s = [s_scr[h] for h in heads]
    for win in range(nw):
        for c in range(nc):
            cs = slice(win * w + c * CHUNK, win * w + (c + 1) * CHUNK)
            for hp in range(0, N_HEADS, 2):
                u = win * N_HEADS + hp
                lhs = [jnp.concatenate([wmat[u + j][c], qmat[u + j][c]], axis=0).astype(BF16) for j in range(2)]
                r = _pair_dot(lhs, [s[hp + j].astype(BF16) for j in range(2)])
                for j in range(2):
                    o_ref[cs, hs[u + j]] = r[j][HEAD_DIM:] + omat[u + j][c]
                    decay_c = jnp.exp(glast[u + j][c * CHUNK:c * CHUNK + 1, :])
                    s[hp + j] = s[hp + j] * decay_c - r[j][:HEAD_DIM] + bmat[u + j][c]
    for h in heads:
        s_scr[h] = s[h]

    @pl.when(step == n_steps - 1)
    def _():
        sfin_ref[0] = s_scr[...]


def _gdn_prompt_call(qkv, bg, bgt, nb, seq_len):
    t, d = qkv.shape[1], N_HEADS * HEAD_DIM
    w = GDN_WINDOW
    nw = GDN_WINDOWS_PER_STEP
    assert seq_len % (w * nw) == 0 and w % CHUNK == 0 and w == HEAD_DIM
    n_steps = seq_len // (w * nw)
    tok = pl.BlockSpec((w * nw, d), lambda b, i: (b * n_steps + i, 0))
    part = lambda k: pl.BlockSpec((N_HEADS, w * nw, HEAD_DIM), lambda b, i: (k, b * n_steps + i, 0))
    kern = functools.partial(_gdn_prompt_kernel, w=w, nw=nw, n_steps=n_steps)
    return pl.pallas_call(
        kern,
        grid=(nb, n_steps),
        in_specs=[part(0), part(1), part(2),
                  pl.BlockSpec((w * nw, LANES), lambda b, i: (b * n_steps + i, 0)),
                  pl.BlockSpec((LANES, w * nw), lambda b, i: (0, b * n_steps + i))],
        out_specs=[tok, pl.BlockSpec((1, N_HEADS, HEAD_DIM, HEAD_DIM), lambda b, i: (b, 0, 0, 0))],
        out_shape=[jax.ShapeDtypeStruct((t, d), F32),
                   jax.ShapeDtypeStruct((nb, N_HEADS, HEAD_DIM, HEAD_DIM), F32)],
        scratch_shapes=[pltpu.VMEM((N_HEADS, HEAD_DIM, HEAD_DIM), F32)],
        compiler_params=_params(2),
        name="gdn_prompt",
    )(qkv, qkv, qkv, bg, bgt)


def _gdn_sample_kernel(q_ref, k_ref, v_ref, bg_ref, s_ref, o_ref, snew_ref, *, bs):
    row = lax.broadcasted_iota(jnp.int32, (HEAD_DIM, HEAD_DIM), 0)
    col = lax.broadcasted_iota(jnp.int32, (HEAD_DIM, HEAD_DIM), 1)
    eye = row == col
    for h in range(N_HEADS):
        hs = slice(h * HEAD_DIM, (h + 1) * HEAD_DIM)
        qh = q_ref[h]
        kh = k_ref[h]
        vh = v_ref[h]
        beta = jnp.broadcast_to(bg_ref[:, h:h + 1], (bs, HEAD_DIM))
        eg = jnp.exp(jnp.broadcast_to(bg_ref[:, N_HEADS + h:N_HEADS + h + 1], (bs, HEAD_DIM)))
        qb = qh.astype(BF16)
        for i in range(bs):
            s = s_ref[i, h]
            kcol = jnp.sum(jnp.where(eye, jnp.broadcast_to(kh[i:i + 1, :], (HEAD_DIM, HEAD_DIM)), 0.0),
                           axis=-1, keepdims=True)
            ks = jnp.sum(kcol * s, axis=0, keepdims=True)
            egi = eg[i:i + 1, :]
            v_new = beta[i:i + 1, :] * (vh[i:i + 1, :] - egi * ks)
            s_new = s * egi + kcol * v_new
            snew_ref[i, h] = s_new
            o_ref[i:i + 1, hs] = _dot(qb, s_new.astype(BF16))[i:i + 1, :]


def _gdn_sample_call(qkv, bg, s0):
    n, d = qkv.shape[1], N_HEADS * HEAD_DIM
    bs = SAMPLE_SEQ_BLOCK
    assert n % bs == 0
    tok = lambda w: pl.BlockSpec((bs, w), lambda i: (i, 0))
    st = pl.BlockSpec((bs, N_HEADS, HEAD_DIM, HEAD_DIM), lambda i: (i, 0, 0, 0))
    return pl.pallas_call(
        functools.partial(_gdn_sample_kernel, bs=bs),
        grid=(n // bs,),
        in_specs=[pl.BlockSpec((N_HEADS, bs, HEAD_DIM), lambda i: (0, i, 0)),
                  pl.BlockSpec((N_HEADS, bs, HEAD_DIM), lambda i: (1, i, 0)),
                  pl.BlockSpec((N_HEADS, bs, HEAD_DIM), lambda i: (2, i, 0)), tok(LANES), st],
        out_specs=[tok(d), st],
        out_shape=[jax.ShapeDtypeStruct((n, d), F32), jax.ShapeDtypeStruct(s0.shape, F32)],
        compiler_params=_params(1),
        name="gdn_sample",
    )(qkv, qkv, qkv, bg, s0)


def _rms(x, w):
    return x * lax.rsqrt(jnp.mean(x * x, axis=-1, keepdims=True) + EPS) * w


def _ln_swish(a, lnw_ref, lnb_ref):
    mu = jnp.mean(a, axis=-1, keepdims=True)
    ac = a - mu
    var = jnp.mean(ac * ac, axis=-1, keepdims=True)
    return _silu(ac * lax.rsqrt(var + EPS) * lnw_ref[...] + lnb_ref[...]).astype(BF16)


def _conv_ln_swish_jobs(ext_scr, cv_scr, a_scr, dww_ref, dwb_ref, lnw_ref, lnb_ref, *, tm, dc):
    n_taps = dww_ref.shape[0]
    rows_per_phase = tm // CONV_PHASES
    first = CONV_HALO - (n_taps - 1)

    def taps(g):
        cols = slice(g * LANES, (g + 1) * LANES)
        acc = [None] * CONV_PHASES
        for start in range(first, first + n_taps + CONV_PHASES - 1):
            src = ext_scr[g, pl.ds(start, rows_per_phase, stride=CONV_PHASES), :]
            for p in range(CONV_PHASES):
                j = start - first - p
                if 0 <= j < n_taps:
                    term = dww_ref[j:j + 1, cols] * src
                    acc[p] = term if acc[p] is None else acc[p] + term
        for p in range(CONV_PHASES):
            cv_scr[g, pl.ds(p, rows_per_phase, stride=CONV_PHASES), :] = acc[p] + dwb_ref[:, cols]

    def finish():
        a = jnp.concatenate([cv_scr[g] for g in range(dc // LANES)], axis=1)
        a_scr[...] = _ln_swish(a, lnw_ref, lnb_ref)

    return [functools.partial(taps, g) for g in range(dc // LANES)] + [finish]


def _fill_conv_history(ext_scr, hist, new, *, tm, dc):
    for g in range(dc // LANES):
        cols = slice(g * LANES, (g + 1) * LANES)
        ext_scr[g, 0:CONV_HALO, :] = hist[:, cols]
        ext_scr[g, CONV_HALO:CONV_HALO + tm, :] = new[:, cols]


def _post_tail(a, x_ref, mod, o_ref, zg_ref, gnw_ref, wca_ref, wgo_ref, wo_ref, n2w_ref,
               w1_ref, w2_ref, fnw_ref, y_ref, *, dff, side_jobs=()):
    side = list(side_jobs)

    def run_side():
        if side:
            side.pop(0)()

    y_a = _dot(a, wca_ref[...])
    run_side()

    gated = []
    for h in range(N_HEADS):
        hs = slice(h * HEAD_DIM, (h + 1) * HEAD_DIM)
        gated.append(_rms(o_ref[:, hs], gnw_ref[...]) * _silu(zg_ref[:, hs].astype(F32)))
    y_b = _dot(jnp.concatenate(gated, axis=1).astype(BF16), wgo_ref[...])
    run_side()

    d = y_a.shape[1]
    merged = zg_ref[:, d:2 * d].astype(F32) * y_a + zg_ref[:, 2 * d:3 * d].astype(F32) * y_b
    x1 = x_ref[...] + mod(2) * _dot(merged.astype(BF16), wo_ref[...])
    run_side()

    h2 = (_rms(x1, n2w_ref[...]) * (1.0 + mod(4)) + mod(3)).astype(BF16)
    step = 1024
    ff = None
    for c in range(dff // step):
        f = jnp.maximum(_dot(h2, w1_ref[:, c * step:(c + 1) * step]), 0.0)
        part = _dot((f * f).astype(BF16), w2_ref[c * step:(c + 1) * step, :])
        ff = part if ff is None else ff + part
        run_side()
    while side:
        run_side()
    x2 = x1 + mod(5) * ff
    y_ref[...] = _rms(x2, fnw_ref[...])


def _post_prompt_kernel(x_ref, mod_ref, glu_ref, glun_ref, dww_ref, dwb_ref, lnw_ref, lnb_ref,
                        o_ref, zg_ref, gnw_ref, wca_ref, wgo_ref, wo_ref, n2w_ref,
                        w1_ref, w2_ref, fnw_ref, y_ref, ext_scr, cv_scr, a_scr,
                        *, tm, tiles_per_seq, d, dc, dff):
    i = pl.program_id(0)
    conv_jobs = functools.partial(_conv_ln_swish_jobs, ext_scr, cv_scr, a_scr, dww_ref, dwb_ref,
                                  lnw_ref, lnb_ref, tm=tm, dc=dc)

    @pl.when(i == 0)
    def _():
        _fill_conv_history(ext_scr, jnp.zeros((CONV_HALO, dc), F32), glu_ref[...], tm=tm, dc=dc)
        for job in conv_jobs():
            job()

    a = a_scr[...]
    hist = jnp.where((i + 1) % tiles_per_seq == 0, 0.0, glu_ref[tm - CONV_HALO:tm, :])
    _fill_conv_history(ext_scr, hist, glun_ref[...], tm=tm, dc=dc)

    mod = lambda k: mod_ref[0, :, k * d:(k + 1) * d]
    _post_tail(a, x_ref, mod, o_ref, zg_ref, gnw_ref, wca_ref, wgo_ref, wo_ref, n2w_ref,
               w1_ref, w2_ref, fnw_ref, y_ref, dff=dff, side_jobs=conv_jobs())


def _post_sample_kernel(x_ref, mod_ref, apre_ref, lnw_ref, lnb_ref,
                        o_ref, zg_ref, gnw_ref, wca_ref, wgo_ref, wo_ref, n2w_ref,
                        w1_ref, w2_ref, fnw_ref, y_ref, *, d, dff):
    a = _ln_swish(apre_ref[...], lnw_ref, lnb_ref)
    mod = lambda k: mod_ref[:, k * d:(k + 1) * d]
    _post_tail(a, x_ref, mod, o_ref, zg_ref, gnw_ref, wca_ref, wgo_ref, wo_ref, n2w_ref,
               w1_ref, w2_ref, fnw_ref, y_ref, dff=dff)


def _post_call(x, mod, conv_in, lnw, lnb, o, zg, gnw, wca, wgo, wo, n2w, w1, w2, fnw, *, seq_len):
    t, d = x.shape
    dc = lnw.shape[1]
    dff = w1.shape[1]
    prompt = len(conv_in) == 3
    tm = min(POST_TILE, t)
    assert t % tm == 0
    n_tiles = t // tm
    tok = lambda w: pl.BlockSpec((tm, w), lambda i: (i, 0))
    common_args = [lnw, lnb, o, zg, gnw, wca, wgo, wo, n2w, w1, w2, fnw]
    common_specs = [_resident(lnw.shape), _resident(lnb.shape), tok(d), tok(3 * d), _resident(gnw.shape),
                    _resident(wca.shape), _resident(wgo.shape), _resident(wo.shape), _resident(n2w.shape),
                    _resident(w1.shape), _resident(w2.shape), _resident(fnw.shape)]
    if prompt:
        glu, dww, dwb = conv_in
        assert seq_len % tm == 0 and tm >= CONV_HALO >= dww.shape[0] - 1
        tiles_per_seq = seq_len // tm
        args = [x, mod, glu, glu, dww, dwb] + common_args
        specs = [tok(d), pl.BlockSpec((1, 1, mod.shape[2]), lambda i: (i // tiles_per_seq, 0, 0)),
                 tok(dc), pl.BlockSpec((tm, dc), lambda i: (jnp.minimum(i + 1, n_tiles - 1), 0)),
                 _resident(dww.shape), _resident(dwb.shape)] + common_specs
        scratch = [pltpu.VMEM((dc // LANES, tm + CONV_HALO, LANES), F32),
                   pltpu.VMEM((dc // LANES, tm, LANES), F32),
                   pltpu.VMEM((tm, dc), BF16)]
        kern = functools.partial(_post_prompt_kernel, tm=tm, tiles_per_seq=tiles_per_seq, d=d, dc=dc, dff=dff)
        name = "post_prompt"
    else:
        args = [x, mod, conv_in[0]] + common_args
        specs = [tok(d), tok(mod.shape[1]), tok(dc)] + common_specs
        scratch = []
        kern = functools.partial(_post_sample_kernel, d=d, dff=dff)
        name = "post_sample"
    return pl.pallas_call(
        kern,
        grid=(n_tiles,),
        in_specs=specs,
        out_specs=tok(d),
        out_shape=jax.ShapeDtypeStruct((t, d), F32),
        scratch_shapes=scratch,
        compiler_params=_params(1),
        name=name,
    )(*args)


def _split_w_in(w_in, d, dc, dqkv):
    w_bf = w_in.astype(BF16)
    o1 = 2 * dc + dqkv + d
    ba = w_bf[:, o1:o1 + 2 * N_HEADS]
    gates = w_bf[:, o1 + 2 * N_HEADS:]
    pad = jnp.zeros((w_bf.shape[0], LANES - 2 * N_HEADS), BF16)
    return w_bf, jnp.concatenate([gates, ba, pad], axis=1)


def _lane_pad(vec, offset):
    out = jnp.zeros((1, LANES), F32)
    return out.at[0, offset:offset + vec.shape[0]].set(vec.astype(F32))


def kernel(x_prompt, x_sample, c_prompt, c_sample, state_conf_conv, state_qkv_conv, state_delta, w_ada, b_ada, norm1_w, w_in, conf_dw_w, conf_dw_b, conf_ln_w, conf_ln_b, w_conf_out, gdn_conv_w, a_log, dt_bias, gdn_norm_w, w_gdn_out, w_o, norm2_w, w_ff1, w_ff2, final_norm_w):
    nb, seq_len, d = x_prompt.shape
    ns = x_sample.shape[0]
    depth = w_ada.shape[0]
    dc = conf_dw_w.shape[2]
    dqkv = gdn_conv_w.shape[2]
    n_hist = conf_dw_w.shape[1] - 1
    n_taps = gdn_conv_w.shape[1]
    assert x_sample.shape[1] == 1

    xp = x_prompt.reshape(nb * seq_len, d)
    xs = x_sample.reshape(ns, d)
    c_all = jnp.concatenate([c_sample, c_prompt], axis=0)
    row2 = lambda v: v.reshape(1, -1).astype(F32)

    conf_p, qkv_p, delta_p, conf_s, qkv_s, delta_s = [], [], [], [], [], []
    for l in range(depth):
        mod = _mod_call(c_all, w_ada[l].astype(F32), row2(b_ada[l]))
        mod_p = mod[ns:].reshape(nb, 1, N_MOD * d)
        mod_s = mod
        w_bf, w_tail = _split_w_in(w_in[l], d, dc, dqkv)
        n1w, n2w, fnw = row2(norm1_w[l]), row2(norm2_w[l]), row2(final_norm_w)
        alog_p = _lane_pad(a_log[l], N_HEADS)
        dtb_p = _lane_pad(dt_bias[l], N_HEADS)
        cw = gdn_conv_w[l].astype(F32)
        dww, dwb = conf_dw_w[l].astype(F32), row2(conf_dw_b[l])
        lnw, lnb = row2(conf_ln_w[l]), row2(conf_ln_b[l])
        gnw = row2(gdn_norm_w[l])
        wca, wgo, wo = w_conf_out[l].astype(BF16), w_gdn_out[l].astype(BF16), w_o[l].astype(BF16)
        w1, w2 = w_ff1[l].astype(BF16), w_ff2[l].astype(BF16)
        assert depth == 1

        glu, qkv, zg, bg, bgt, tail = _pre_prompt_call(
            xp, mod_p, n1w, w_bf, w_tail, cw, alog_p, dtb_p, seq_len, dc)
        o, s_fin = _gdn_prompt_call(qkv, bg, bgt, nb, seq_len)
        xp = _post_call(xp, mod_p, (glu, dww, dwb), lnw, lnb, o, zg, gnw, wca, wgo, wo, n2w, w1, w2, fnw,
                        seq_len=seq_len)
        conf_p.append(glu.reshape(nb, seq_len, dc)[:, seq_len - n_hist:])
        qkv_p.append(tail.reshape(nb, SUBLANES, dqkv)[:, SUBLANES - (n_taps - 1):])
        delta_p.append(s_fin)

        sconf = jnp.transpose(state_conf_conv[l].astype(F32), (1, 0, 2))
        sqkv = jnp.transpose(state_qkv_conv[l].astype(F32), (1, 0, 2))
        apre, qkv, zg, bg, nconf, nqkv = _pre_sample_call(
            xs, mod_s, n1w, w_bf, w_tail, cw, alog_p, dtb_p, sconf, sqkv, dww, dwb)
        o, s_new = _gdn_sample_call(qkv, bg, state_delta[l].astype(F32))
        xs = _post_call(xs, mod_s, (apre,), lnw, lnb, o, zg, gnw, wca, wgo, wo, n2w, w1, w2, fnw,
                        seq_len=1)
        conf_s.append(jnp.transpose(nconf, (1, 0, 2)))
        qkv_s.append(jnp.transpose(nqkv, (1, 0, 2)))
        delta_s.append(s_new)

    return (xp.reshape(nb, seq_len, d), xs.reshape(ns, 1, d),
            jnp.stack(conf_p), jnp.stack(qkv_p), jnp.stack(delta_p),
            jnp.stack(conf_s), jnp.stack(qkv_s), jnp.stack(delta_s))
```

```python
import functools

import jax
import jax.numpy as jnp
from jax import lax
from jax.experimental import pallas as pl
from jax.experimental.pallas import tpu as pltpu

F32 = jnp.float32
BF16 = jnp.bfloat16

EPS = 1e-6
N_MOD = 6
N_HEADS = 8
HEAD_DIM = 128
CHUNK = 64
LANES = 128
SUBLANES = 8
VMEM_LIMIT_BYTES = 56 * 1024 * 1024

PRE_TILE = 512
POST_TILE = 256
GDN_WINDOW = 128
GDN_WINDOWS_PER_STEP = 4
CONV_HALO = 32
SAMPLE_SEQ_BLOCK = 16
CONV_PHASES = 4


NEG_LOG2_E = -1.4426950408889634


def _sigmoid(x):
    return 1.0 / (1.0 + jnp.exp2(x * NEG_LOG2_E))


def _silu(x):
    return x * _sigmoid(x)


def _softplus(x):
    return jnp.maximum(x, 0.0) + jnp.log(1.0 + jnp.exp(-jnp.abs(x)))


def _dot(a, b):
    return jnp.dot(a, b, preferred_element_type=F32)


def _dot_nt(a, b):
    return lax.dot_general(a, b, (((1,), (1,)), ((), ())), preferred_element_type=F32)


def _resident(shape):
    n = len(shape)
    return pl.BlockSpec(shape, lambda *_: (0,) * n, pipeline_mode=pl.Buffered(1))


def _params(n_grid):
    return pltpu.CompilerParams(dimension_semantics=("arbitrary",) * n_grid,
                                vmem_limit_bytes=VMEM_LIMIT_BYTES)


def _mod_kernel(c_ref, w_ref, b_ref, o_ref):
    c = c_ref[...]
    o_ref[...] = _dot(_silu(c).astype(BF16), w_ref[...].astype(BF16)) + b_ref[...]


def _mod_call(c_all, w_ada, b_ada):
    n, d = c_all.shape
    e = w_ada.shape[1]
    tn = e // 8
    return pl.pallas_call(
        _mod_kernel,
        grid=(e // tn,),
        in_specs=[pl.BlockSpec((n, d), lambda j: (0, 0)),
                  pl.BlockSpec((d, tn), lambda j: (0, j)),
                  pl.BlockSpec((1, tn), lambda j: (0, j))],
        out_specs=pl.BlockSpec((n, tn), lambda j: (0, j)),
        out_shape=jax.ShapeDtypeStruct((n, e), F32),
        compiler_params=_params(1),
        name="mod",
    )(c_all, w_ada, b_ada)


def _pre_front(x, shift1, scale1, n1w):
    ms = jnp.mean(x * x, axis=-1, keepdims=True)
    h = x * lax.rsqrt(ms + EPS) * n1w
    return (h * (1.0 + scale1) + shift1).astype(BF16)


def _qkv_finish(y, g, n_groups):
    s = _silu(y)
    n_qk = n_groups // 3
    if g < 2 * n_qk:
        s = s * lax.rsqrt(jnp.sum(s * s, axis=-1, keepdims=True) + EPS)
    if g < n_qk:
        s = s * (HEAD_DIM ** -0.5)
    return s


def _beta_decay(ba, alog, dtb):
    lane = lax.broadcasted_iota(jnp.int32, ba.shape, 1)
    beta = _sigmoid(ba)
    g = -jnp.exp(alog) * _softplus(ba + dtb)
    return jnp.where(lane < N_HEADS, beta, g)


def _pre_prompt_kernel(x_ref, mod_ref, n1w_ref, w_ref, wt_ref, cw_ref, alog_ref, dtb_ref, cum_ref,
                       glu_ref, qkv_ref, zg_ref, bg_ref, bgt_ref, tail_ref,
                       qkv_scr, h_scr, *, tm, tiles_per_seq, d, dc, dqkv):
    i = pl.program_id(0)
    h_scr[...] = _pre_front(x_ref[...], mod_ref[0, :, 0:d], mod_ref[0, :, d:2 * d], n1w_ref[...])

    u = _dot(h_scr[...], w_ref[:, 0:2 * dc])
    glu_ref[...] = u[:, :dc] * _sigmoid(u[:, dc:])

    @pl.when(i % tiles_per_seq == 0)
    def _():
        qkv_scr[:, 0:SUBLANES, :] = jnp.zeros((dqkv // LANES, SUBLANES, LANES), F32)

    o_qkv = 2 * dc
    o_z = o_qkv + dqkv
    step = 512
    n_chunks = dqkv // step
    n_taps = cw_ref.shape[0]

    n_groups = dqkv // LANES
    groups_per_chunk = step // LANES
    rows_per_phase = tm // CONV_PHASES

    def project(c):
        r = _dot(h_scr[...], w_ref[:, o_qkv + c * step:o_qkv + (c + 1) * step])
        for k in range(groups_per_chunk):
            g = c * groups_per_chunk + k
            qkv_scr[g, SUBLANES:SUBLANES + tm, :] = r[:, k * LANES:(k + 1) * LANES]
            tail_ref[:, g * LANES:(g + 1) * LANES] = qkv_scr[g, tm:tm + SUBLANES, :]

    def conv(c):
        for g in range(c * groups_per_chunk, (c + 1) * groups_per_chunk):
            cols = slice(g * LANES, (g + 1) * LANES)
            first = SUBLANES - (n_taps - 1)
            slabs = {start: qkv_scr[g, pl.ds(start, rows_per_phase, stride=CONV_PHASES), :]
                     for start in range(first, first + n_taps + CONV_PHASES - 1)}
            for p in range(CONV_PHASES):
                y = None
                for j in range(n_taps):
                    term = cw_ref[j:j + 1, cols] * slabs[first + j + p]
                    y = term if y is None else y + term
                qkv_ref[g, pl.ds(p, rows_per_phase, stride=CONV_PHASES), :] = _qkv_finish(y, g, n_groups)
            qkv_scr[g, 0:SUBLANES, :] = qkv_scr[g, tm:tm + SUBLANES, :]

    pieces = [(wsrc, o_w + half * step, k * d + half * step, act)
              for k, (wsrc, o_w, act) in enumerate([(w_ref, o_z, False), (wt_ref, 0, True), (wt_ref, d, True)])
              for half in range(d // step)]
    assert len(pieces) == n_chunks

    def side(c):
        wsrc, o_w, o_out, act = pieces[c]
        r = _dot(h_scr[...], wsrc[:, o_w:o_w + step])
        zg_ref[:, o_out:o_out + step] = (_sigmoid(r) if act else r).astype(BF16)

    project(0)
    for c in range(n_chunks):
        if c + 1 < n_chunks:
            project(c + 1)
        side(c)
        conv(c)

    ba = _dot(h_scr[...], wt_ref[:, 2 * d:2 * d + LANES])
    bg = _beta_decay(ba, alog_ref[...], dtb_ref[...])
    bg_hi = bg.astype(BF16)
    bg_lo = (bg - bg_hi.astype(F32)).astype(BF16)
    cum = _dot(cum_ref[...], bg_hi) + _dot(cum_ref[...], bg_lo)
    lane = lax.broadcasted_iota(jnp.int32, bg.shape, 1)
    bg = jnp.where(lane < N_HEADS, bg, cum)
    bg_ref[...] = bg
    bgt_ref[...] = bg.T


def _pre_prompt_call(x, mod_p, n1w, w_bf, w_tail, cw, alog_p, dtb_p, seq_len, dc):
    t, d = x.shape
    tm = PRE_TILE
    assert seq_len % tm == 0 and tm % CHUNK == 0 and tm % (CONV_PHASES * SUBLANES) == 0
    dqkv = cw.shape[1]
    w_main_shape = (d, 2 * dc + dqkv + d + LANES)
    nb = t // seq_len
    tiles_per_seq = seq_len // tm
    row = lax.broadcasted_iota(jnp.int32, (tm, tm), 0)
    col = lax.broadcasted_iota(jnp.int32, (tm, tm), 1)
    cum_mat = ((row // CHUNK == col // CHUNK) & (col <= row)).astype(BF16)

    tok = lambda w: pl.BlockSpec((tm, w), lambda i: (i, 0))
    kern = functools.partial(_pre_prompt_kernel, tm=tm, tiles_per_seq=tiles_per_seq, d=d, dc=dc, dqkv=dqkv)
    return pl.pallas_call(
        kern,
        grid=(t // tm,),
        in_specs=[tok(d),
                  pl.BlockSpec((1, 1, mod_p.shape[2]), lambda i: (i // tiles_per_seq, 0, 0)),
                  _resident(n1w.shape), _resident(w_main_shape), _resident(w_tail.shape), _resident(cw.shape),
                  _resident(alog_p.shape), _resident(dtb_p.shape), _resident(cum_mat.shape)],
        out_specs=[tok(dc), pl.BlockSpec((dqkv // LANES, tm, LANES), lambda i: (0, i, 0)), tok(3 * d), tok(LANES),
                   pl.BlockSpec((LANES, tm), lambda i: (0, i)),
                   pl.BlockSpec((SUBLANES, dqkv), lambda i: (i // tiles_per_seq, 0))],
        out_shape=[jax.ShapeDtypeStruct((t, dc), F32),
                   jax.ShapeDtypeStruct((dqkv // LANES, t, LANES), F32), jax.ShapeDtypeStruct((t, 3 * d), BF16),
                   jax.ShapeDtypeStruct((t, LANES), F32),
                   jax.ShapeDtypeStruct((LANES, t), F32),
                   jax.ShapeDtypeStruct((nb * SUBLANES, dqkv), F32)],
        scratch_shapes=[pltpu.VMEM((dqkv // LANES, tm + SUBLANES, LANES), F32), pltpu.VMEM((tm, d), BF16)],
        compiler_params=_params(1),
        name="pre_prompt",
    )(x, mod_p, n1w, w_bf, w_tail, cw, alog_p, dtb_p, cum_mat)


def _pre_sample_kernel(x_ref, mod_ref, n1w_ref, w_ref, wt_ref, cw_ref, alog_ref, dtb_ref,
                       sconf_ref, sqkv_ref, dww_ref, dwb_ref,
                       apre_ref, qkv_ref, zg_ref, bg_ref,
                       nconf_ref, nqkv_ref, *, d, dc, dqkv):
    hb = _pre_front(x_ref[...], mod_ref[:, 0:d], mod_ref[:, d:2 * d], n1w_ref[...])

    u = _dot(hb, w_ref[:, 0:2 * dc])
    glu = u[:, :dc] * _sigmoid(u[:, dc:])
    n_hist = sconf_ref.shape[0]
    acc = dww_ref[n_hist:n_hist + 1, :] * glu + dwb_ref[...]
    for j in range(n_hist):
        row = sconf_ref[j]
        acc = acc + dww_ref[j:j + 1, :] * row
        if j > 0:
            nconf_ref[j - 1] = row
    apre_ref[...] = acc
    nconf_ref[n_hist - 1] = glu

    o_qkv = 2 * dc
    n_taps = cw_ref.shape[0]
    for g in range(dqkv // LANES):
        cols = slice(g * LANES, (g + 1) * LANES)
        raw = _dot(hb, w_ref[:, o_qkv + g * LANES:o_qkv + (g + 1) * LANES])
        nqkv_ref[n_taps - 2, :, cols] = raw
        y = cw_ref[n_taps - 1:n_taps, cols] * raw
        for j in range(n_taps - 1):
            row = sqkv_ref[j, :, cols]
            y = y + cw_ref[j:j + 1, cols] * row
            if j > 0:
                nqkv_ref[j - 1, :, cols] = row
        qkv_ref[g] = _qkv_finish(y, g, dqkv // LANES)

    o_z = o_qkv + dqkv
    zg_ref[:, 0:d] = _dot(hb, w_ref[:, o_z:o_z + d]).astype(BF16)
    zg_ref[:, d:2 * d] = _sigmoid(_dot(hb, wt_ref[:, 0:d])).astype(BF16)
    zg_ref[:, 2 * d:3 * d] = _sigmoid(_dot(hb, wt_ref[:, d:2 * d])).astype(BF16)
    ba = _dot(hb, wt_ref[:, 2 * d:2 * d + LANES])
    bg_ref[...] = _beta_decay(ba, alog_ref[...], dtb_ref[...])


def _pre_sample_call(x, mod_s, n1w, w_bf, w_tail, cw, alog_p, dtb_p, sconf, sqkv, dww, dwb):
    n, d = x.shape
    dqkv = cw.shape[1]
    dc = dwb.shape[1]
    w_main_shape = (d, 2 * dc + dqkv + d + LANES)
    tm = 32
    assert n % tm == 0
    tok = lambda w: pl.BlockSpec((tm, w), lambda i: (i, 0))
    state = lambda a: pl.BlockSpec((a.shape[0], tm, a.shape[2]), lambda i: (0, i, 0))
    kern = functools.partial(_pre_sample_kernel, d=d, dc=dc, dqkv=dqkv)
    return pl.pallas_call(
        kern,
        grid=(n // tm,),
        in_specs=[tok(d), tok(mod_s.shape[1]),
                  _resident(n1w.shape), _resident(w_main_shape), _resident(w_tail.shape), _resident(cw.shape),
                  _resident(alog_p.shape), _resident(dtb_p.shape),
                  state(sconf), state(sqkv),
                  _resident(dww.shape), _resident(dwb.shape)],
        out_specs=[tok(dc), pl.BlockSpec((dqkv // LANES, tm, LANES), lambda i: (0, i, 0)), tok(3 * d), tok(LANES),
                   state(sconf), state(sqkv)],
        out_shape=[jax.ShapeDtypeStruct((n, dc), F32),
                   jax.ShapeDtypeStruct((dqkv // LANES, n, LANES), F32), jax.ShapeDtypeStruct((n, 3 * d), BF16),
                   jax.ShapeDtypeStruct((n, LANES), F32),
                   jax.ShapeDtypeStruct(sconf.shape, F32), jax.ShapeDtypeStruct(sqkv.shape, F32)],
        compiler_params=_params(1),
        name="pre_sample",
    )(x, mod_s, n1w, w_bf, w_tail, cw, alog_p, dtb_p, sconf, sqkv, dww, dwb)


def _half_block_mask(row, col, size):
    shift = size.bit_length() - 1
    half = size // 2
    return ((lax.shift_right_logical(row, shift) == lax.shift_right_logical(col, shift))
            & ((row & half) != 0) & ((col & half) == 0))


def _block_diag2(a, b):
    z = jnp.zeros(a.shape, a.dtype)
    return jnp.concatenate([jnp.concatenate([a, z], axis=1), jnp.concatenate([z, b], axis=1)], axis=0)


def _pair_dot(a, b):
    n = b[0].shape[1]
    out = _dot(jnp.concatenate(a, axis=1), _block_diag2(b[0], b[1]))
    return [out[:, :n], out[:, n:]]


def _gdn_prompt_kernel(q_ref, k_ref, v_ref, bg_ref, bgt_ref, o_ref, sfin_ref, s_scr, *, w, nw, n_steps):
    step = pl.program_id(1)

    @pl.when(step == 0)
    def _():
        s_scr[...] = jnp.zeros(s_scr.shape, F32)

    nc = w // CHUNK
    heads = range(N_HEADS)
    units = [(win, h) for win in range(nw) for h in heads]
    n_units = len(units)
    upairs = range(0, n_units, 2)
    row = lax.broadcasted_iota(jnp.int32, (w, w), 0)
    col = lax.broadcasted_iota(jnp.int32, (w, w), 1)
    shift = CHUNK.bit_length() - 1
    same = lax.shift_right_logical(row, shift) == lax.shift_right_logical(col, shift)
    m_incl = same & (col <= row)
    m_strict = same & (col < row)
    eye = (row == col).astype(F32)

    rs = [slice(win * w, (win + 1) * w) for win, _ in units]
    hs = [slice(h * HEAD_DIM, (h + 1) * HEAD_DIM) for _, h in units]
    qh = [q_ref[h, rs[u], :] for u, (_, h) in enumerate(units)]
    kh = [k_ref[h, rs[u], :] for u, (_, h) in enumerate(units)]
    vh = [v_ref[h, rs[u], :] for u, (_, h) in enumerate(units)]
    bcol, gcb, decay = [], [], []
    for u, (win, h) in enumerate(units):
        bcol.append(jnp.broadcast_to(bg_ref[rs[u], h:h + 1], (w, HEAD_DIM)))
        gcb.append(jnp.broadcast_to(bg_ref[rs[u], N_HEADS + h:N_HEADS + h + 1], (w, HEAD_DIM)))
        grow = bgt_ref[N_HEADS + h:N_HEADS + h + 1, rs[u]]
        decay.append(jnp.exp(jnp.where(m_incl, gcb[u] - grow, -1e30)))

    kb = [kh[u].astype(BF16) for u in range(n_units)]
    qb = [qh[u].astype(BF16) for u in range(n_units)]
    amat, att = [None] * n_units, [None] * n_units
    for p in upairs:
        lhs = jnp.concatenate([jnp.concatenate([kb[p], kb[p + 1]], axis=1),
                               jnp.concatenate([qb[p], qb[p + 1]], axis=1)], axis=0)
        r = _dot_nt(lhs, _block_diag2(kb[p], kb[p + 1]))
        for j in range(2):
            kk = r[:w, j * w:(j + 1) * w]
            amat[p + j] = jnp.where(m_strict, kk * bcol[p + j] * decay[p + j], 0.0)
            att[p + j] = (r[w:, j * w:(j + 1) * w] * decay[p + j]).astype(BF16)

    x = [eye - jnp.where(_half_block_mask(row, col, 2), amat[u], 0.0) for u in range(n_units)]
    size = 4
    while size <= CHUNK:
        mask = _half_block_mask(row, col, size)
        for p in upairs:
            blk = [jnp.where(mask, amat[p + j], 0.0).astype(BF16) for j in range(2)]
            xb = [x[p + j].astype(BF16) for j in range(2)]
            y = _pair_dot(blk, xb)
            upd = _pair_dot(xb, [y[j].astype(BF16) for j in range(2)])
            for j in range(2):
                x[p + j] = x[p + j] - upd[j]
        size *= 2

    eg = [jnp.exp(gcb[u]) for u in range(n_units)]
    kv = []
    for u in range(n_units):
        rhs = jnp.concatenate([kh[u] * (bcol[u] * eg[u]), vh[u] * bcol[u]], axis=1).astype(BF16)
        kv.append(_dot(x[u].astype(BF16), rhs).astype(BF16))
    qe, glast, ktt = [], [], []
    for u in range(n_units):
        qe.append(qh[u] * eg[u])
        gl = jnp.concatenate(
            [jnp.broadcast_to(gcb[u][(c + 1) * CHUNK - 1:(c + 1) * CHUNK, :], (CHUNK, HEAD_DIM)) for c in range(nc)],
            axis=0)
        glast.append(gl)
        ktt.append((kh[u] * jnp.exp(gl - gcb[u])).T.astype(BF16))
    wmat = [[None] * nc for _ in range(n_units)]
    bmat = [[None] * nc for _ in range(n_units)]
    qmat = [[None] * nc for _ in range(n_units)]
    omat = [[None] * nc for _ in range(n_units)]
    zeros = jnp.zeros((CHUNK, 2 * HEAD_DIM), BF16)
    for c in range(nc):
        cs = slice(c * CHUNK, (c + 1) * CHUNK)
        for u in range(n_units):
            parts = [zeros] * nc
            parts[c] = kv[u][cs]
            kv_c = jnp.concatenate(parts, axis=0)
            r = _dot(jnp.concatenate([ktt[u], att[u][cs]], axis=0), kv_c)
            wmat[u][c] = r[:HEAD_DIM, :HEAD_DIM]
            bmat[u][c] = r[:HEAD_DIM, HEAD_DIM:]
            qmat[u][c] = qe[u][cs] - r[HEAD_DIM:, :HEAD_DIM]
            omat[u][c] = r[HEAD_DIM:, HEAD_DIM:]

    s = [s_scr[h] for h in heads]
    for win in range(nw):
        for c in range(nc):
            cs = slice(win * w + c * CHUNK, win * w + (c + 1) * CHUNK)
            for hp in range(0, N_HEADS, 2):
                u = win * N_HEADS + hp
                lhs = [jnp.concatenate([wmat[u + j][c], qmat[u + j][c]], axis=0).astype(BF16) for j in range(2)]
                r = _pair_dot(lhs, [s[hp + j].astype(BF16) for j in range(2)])
                for j in range(2):
                    o_ref[cs, hs[u + j]] = r[j][HEAD_DIM:] + omat[u + j][c]
                    decay_c = jnp.exp(glast[u + j][c * CHUNK:c * CHUNK + 1, :])
                    s[hp + j] = s[hp + j] * decay_c - r[j][:HEAD_DIM] + bmat[u + j][c]
    for h in heads:
        s_scr[h] = s[h]

    @pl.when(step == n_steps - 1)
    def _():
        sfin_ref[0] = s_scr[...]


def _gdn_prompt_call(qkv, bg, bgt, nb, seq_len):
    t, d = qkv.shape[1], N_HEADS * HEAD_DIM
    w = GDN_WINDOW
    nw = GDN_WINDOWS_PER_STEP
    assert seq_len % (w * nw) == 0 and w % CHUNK == 0 and w == HEAD_DIM
    n_steps = seq_len // (w * nw)
    tok = pl.BlockSpec((w * nw, d), lambda b, i: (b * n_steps + i, 0))
    part = lambda k: pl.BlockSpec((N_HEADS, w * nw, HEAD_DIM), lambda b, i: (k, b * n_steps + i, 0))
    kern = functools.partial(_gdn_prompt_kernel, w=w, nw=nw, n_steps=n_steps)
    return pl.pallas_call(
        kern,
        grid=(nb, n_steps),
        in_specs=[part(0), part(1), part(2),
                  pl.BlockSpec((w * nw, LANES), lambda b, i: (b * n_steps + i, 0)),
                  pl.BlockSpec((LANES, w * nw), lambda b, i: (0, b * n_steps + i))],
        out_specs=[tok, pl.BlockSpec((1, N_HEADS, HEAD_DIM, HEAD_DIM), lambda b, i: (b, 0, 0, 0))],
        out_shape=[jax.ShapeDtypeStruct((t, d), F32),
                   jax.ShapeDtypeStruct((nb, N_HEADS, HEAD_DIM, HEAD_DIM), F32)],
        scratch_shapes=[pltpu.VMEM((N_HEADS, HEAD_DIM, HEAD_DIM), F32)],
        compiler_params=_params(2),
        name="gdn_prompt",
    )(qkv, qkv, qkv, bg, bgt)


def _gdn_sample_kernel(q_ref, k_ref, v_ref, bg_ref, s_ref, o_ref, snew_ref, *, bs):
    row = lax.broadcasted_iota(jnp.int32, (HEAD_DIM, HEAD_DIM), 0)
    col = lax.broadcasted_iota(jnp.int32, (HEAD_DIM, HEAD_DIM), 1)
    eye = row == col
    for h in range(N_HEADS):
        hs = slice(h * HEAD_DIM, (h + 1) * HEAD_DIM)
        qh = q_ref[h]
        kh = k_ref[h]
        vh = v_ref[h]
        beta = jnp.broadcast_to(bg_ref[:, h:h + 1], (bs, HEAD_DIM))
        eg = jnp.exp(jnp.broadcast_to(bg_ref[:, N_HEADS + h:N_HEADS + h + 1], (bs, HEAD_DIM)))
        qb = qh.astype(BF16)
        for i in range(bs):
            s = s_ref[i, h]
            kcol = jnp.sum(jnp.where(eye, jnp.broadcast_to(kh[i:i + 1, :], (HEAD_DIM, HEAD_DIM)), 0.0),
                           axis=-1, keepdims=True)
            ks = jnp.sum(kcol * s, axis=0, keepdims=True)
            egi = eg[i:i + 1, :]
            v_new = beta[i:i + 1, :] * (vh[i:i + 1, :] - egi * ks)
            s_new = s * egi + kcol * v_new
            snew_ref[i, h] = s_new
            o_ref[i:i + 1, hs] = _dot(qb, s_new.astype(BF16))[i:i + 1, :]


def _gdn_sample_call(qkv, bg, s0):
    n, d = qkv.shape[1], N_HEADS * HEAD_DIM
    bs = SAMPLE_SEQ_BLOCK
    assert n % bs == 0
    tok = lambda w: pl.BlockSpec((bs, w), lambda i: (i, 0))
    st = pl.BlockSpec((bs, N_HEADS, HEAD_DIM, HEAD_DIM), lambda i: (i, 0, 0, 0))
    return pl.pallas_call(
        functools.partial(_gdn_sample_kernel, bs=bs),
        grid=(n // bs,),
        in_specs=[pl.BlockSpec((N_HEADS, bs, HEAD_DIM), lambda i: (0, i, 0)),
                  pl.BlockSpec((N_HEADS, bs, HEAD_DIM), lambda i: (1, i, 0)),
                  pl.BlockSpec((N_HEADS, bs, HEAD_DIM), lambda i: (2, i, 0)), tok(LANES), st],
        out_specs=[tok(d), st],
        out_shape=[jax.ShapeDtypeStruct((n, d), F32), jax.ShapeDtypeStruct(s0.shape, F32)],
        compiler_params=_params(1),
        name="gdn_sample",
    )(qkv, qkv, qkv, bg, s0)


def _rms(x, w):
    return x * lax.rsqrt(jnp.mean(x * x, axis=-1, keepdims=True) + EPS) * w


def _ln_swish(a, lnw_ref, lnb_ref):
    mu = jnp.mean(a, axis=-1, keepdims=True)
    ac = a - mu
    var = jnp.mean(ac * ac, axis=-1, keepdims=True)
    return _silu(ac * lax.rsqrt(var + EPS) * lnw_ref[...] + lnb_ref[...]).astype(BF16)


def _conv_ln_swish_jobs(ext_scr, cv_scr, a_scr, dww_ref, dwb_ref, lnw_ref, lnb_ref, *, tm, dc):
    n_taps = dww_ref.shape[0]
    rows_per_phase = tm // CONV_PHASES
    first = CONV_HALO - (n_taps - 1)

    def taps(g):
        cols = slice(g * LANES, (g + 1) * LANES)
        acc = [None] * CONV_PHASES
        for start in range(first, first + n_taps + CONV_PHASES - 1):
            src = ext_scr[g, pl.ds(start, rows_per_phase, stride=CONV_PHASES), :]
            for p in range(CONV_PHASES):
                j = start - first - p
                if 0 <= j < n_taps:
                    term = dww_ref[j:j + 1, cols] * src
                    acc[p] = term if acc[p] is None else acc[p] + term
        for p in range(CONV_PHASES):
            cv_scr[g, pl.ds(p, rows_per_phase, stride=CONV_PHASES), :] = acc[p] + dwb_ref[:, cols]

    def finish():
        a = jnp.concatenate([cv_scr[g] for g in range(dc // LANES)], axis=1)
        a_scr[...] = _ln_swish(a, lnw_ref, lnb_ref)

    return [functools.partial(taps, g) for g in range(dc // LANES)] + [finish]


def _fill_conv_history(ext_scr, hist, new, *, tm, dc):
    for g in range(dc // LANES):
        cols = slice(g * LANES, (g + 1) * LANES)
        ext_scr[g, 0:CONV_HALO, :] = hist[:, cols]
        ext_scr[g, CONV_HALO:CONV_HALO + tm, :] = new[:, cols]


def _post_tail(a, x_ref, mod, o_ref, zg_ref, gnw_ref, wca_ref, wgo_ref, wo_ref, n2w_ref,
               w1_ref, w2_ref, fnw_ref, y_ref, *, dff, side_jobs=()):
    side = list(side_jobs)

    def run_side():
        if side:
            side.pop(0)()

    y_a = _dot(a, wca_ref[...])
    run_side()

    gated = []
    for h in range(N_HEADS):
        hs = slice(h * HEAD_DIM, (h + 1) * HEAD_DIM)
        gated.append(_rms(o_ref[:, hs], gnw_ref[...]) * _silu(zg_ref[:, hs].astype(F32)))
    y_b = _dot(jnp.concatenate(gated, axis=1).astype(BF16), wgo_ref[...])
    run_side()

    d = y_a.shape[1]
    merged = zg_ref[:, d:2 * d].astype(F32) * y_a + zg_ref[:, 2 * d:3 * d].astype(F32) * y_b
    x1 = x_ref[...] + mod(2) * _dot(merged.astype(BF16), wo_ref[...])
    run_side()

    h2 = (_rms(x1, n2w_ref[...]) * (1.0 + mod(4)) + mod(3)).astype(BF16)
    step = 1024
    ff = None
    for c in range(dff // step):
        f = jnp.maximum(_dot(h2, w1_ref[:, c * step:(c + 1) * step]), 0.0)
        part = _dot((f * f).astype(BF16), w2_ref[c * step:(c + 1) * step, :])
        ff = part if ff is None else ff + part
        run_side()
    while side:
        run_side()
    x2 = x1 + mod(5) * ff
    y_ref[...] = _rms(x2, fnw_ref[...])


def _post_prompt_kernel(x_ref, mod_ref, glu_ref, glun_ref, dww_ref, dwb_ref, lnw_ref, lnb_ref,
                        o_ref, zg_ref, gnw_ref, wca_ref, wgo_ref, wo_ref, n2w_ref,
                        w1_ref, w2_ref, fnw_ref, y_ref, ext_scr, cv_scr, a_scr,
                        *, tm, tiles_per_seq, d, dc, dff):
    i = pl.program_id(0)
    conv_jobs = functools.partial(_conv_ln_swish_jobs, ext_scr, cv_scr, a_scr, dww_ref, dwb_ref,
                                  lnw_ref, lnb_ref, tm=tm, dc=dc)

    @pl.when(i == 0)
    def _():
        _fill_conv_history(ext_scr, jnp.zeros((CONV_HALO, dc), F32), glu_ref[...], tm=tm, dc=dc)
        for job in conv_jobs():
            job()

    a = a_scr[...]
    hist = jnp.where((i + 1) % tiles_per_seq == 0, 0.0, glu_ref[tm - CONV_HALO:tm, :])
    _fill_conv_history(ext_scr, hist, glun_ref[...], tm=tm, dc=dc)

    mod = lambda k: mod_ref[0, :, k * d:(k + 1) * d]
    _post_tail(a, x_ref, mod, o_ref, zg_ref, gnw_ref, wca_ref, wgo_ref, wo_ref, n2w_ref,
               w1_ref, w2_ref, fnw_ref, y_ref, dff=dff, side_jobs=conv_jobs())


def _post_sample_kernel(x_ref, mod_ref, apre_ref, lnw_ref, lnb_ref,
                        o_ref, zg_ref, gnw_ref, wca_ref, wgo_ref, wo_ref, n2w_ref,
                        w1_ref, w2_ref, fnw_ref, y_ref, *, d, dff):
    a = _ln_swish(apre_ref[...], lnw_ref, lnb_ref)
    mod = lambda k: mod_ref[:, k * d:(k + 1) * d]
    _post_tail(a, x_ref, mod, o_ref, zg_ref, gnw_ref, wca_ref, wgo_ref, wo_ref, n2w_ref,
               w1_ref, w2_ref, fnw_ref, y_ref, dff=dff)


def _post_call(x, mod, conv_in, lnw, lnb, o, zg, gnw, wca, wgo, wo, n2w, w1, w2, fnw, *, seq_len):
    t, d = x.shape
    dc = lnw.shape[1]
    dff = w1.shape[1]
    prompt = len(conv_in) == 3
    tm = min(POST_TILE, t)
    assert t % tm == 0
    n_tiles = t // tm
    tok = lambda w: pl.BlockSpec((tm, w), lambda i: (i, 0))
    common_args = [lnw, lnb, o, zg, gnw, wca, wgo, wo, n2w, w1, w2, fnw]
    common_specs = [_resident(lnw.shape), _resident(lnb.shape), tok(d), tok(3 * d), _resident(gnw.shape),
                    _resident(wca.shape), _resident(wgo.shape), _resident(wo.shape), _resident(n2w.shape),
                    _resident(w1.shape), _resident(w2.shape), _resident(fnw.shape)]
    if prompt:
        glu, dww, dwb = conv_in
        assert seq_len % tm == 0 and tm >= CONV_HALO >= dww.shape[0] - 1
        tiles_per_seq = seq_len // tm
        args = [x, mod, glu, glu, dww, dwb] + common_args
        specs = [tok(d), pl.BlockSpec((1, 1, mod.shape[2]), lambda i: (i // tiles_per_seq, 0, 0)),
                 tok(dc), pl.BlockSpec((tm, dc), lambda i: (jnp.minimum(i + 1, n_tiles - 1), 0)),
                 _resident(dww.shape), _resident(dwb.shape)] + common_specs
        scratch = [pltpu.VMEM((dc // LANES, tm + CONV_HALO, LANES), F32),
                   pltpu.VMEM((dc // LANES, tm, LANES), F32),
                   pltpu.VMEM((tm, dc), BF16)]
        kern = functools.partial(_post_prompt_kernel, tm=tm, tiles_per_seq=tiles_per_seq, d=d, dc=dc, dff=dff)
        name = "post_prompt"
    else:
        args = [x, mod, conv_in[0]] + common_args
        specs = [tok(d), tok(mod.shape[1]), tok(dc)] + common_specs
        scratch = []
        kern = functools.partial(_post_sample_kernel, d=d, dff=dff)
        name = "post_sample"
    return pl.pallas_call(
        kern,
        grid=(n_tiles,),
        in_specs=specs,
        out_specs=tok(d),
        out_shape=jax.ShapeDtypeStruct((t, d), F32),
        scratch_shapes=scratch,
        compiler_params=_params(1),
        name=name,
    )(*args)


def _split_w_in(w_in, d, dc, dqkv):
    w_bf = w_in.astype(BF16)
    o1 = 2 * dc + dqkv + d
    ba = w_bf[:, o1:o1 + 2 * N_HEADS]
    gates = w_bf[:, o1 + 2 * N_HEADS:]
    pad = jnp.zeros((w_bf.shape[0], LANES - 2 * N_HEADS), BF16)
    return w_bf, jnp.concatenate([gates, ba, pad], axis=1)


def _lane_pad(vec, offset):
    out = jnp.zeros((1, LANES), F32)
    return out.at[0, offset:offset + vec.shape[0]].set(vec.astype(F32))


def kernel(x_prompt, x_sample, c_prompt, c_sample, state_conf_conv, state_qkv_conv, state_delta, w_ada, b_ada, norm1_w, w_in, conf_dw_w, conf_dw_b, conf_ln_w, conf_ln_b, w_conf_out, gdn_conv_w, a_log, dt_bias, gdn_norm_w, w_gdn_out, w_o, norm2_w, w_ff1, w_ff2, final_norm_w):
    nb, seq_len, d = x_prompt.shape
    ns = x_sample.shape[0]
    depth = w_ada.shape[0]
    dc = conf_dw_w.shape[2]
    dqkv = gdn_conv_w.shape[2]
    n_hist = conf_dw_w.shape[1] - 1
    n_taps = gdn_conv_w.shape[1]
    assert x_sample.shape[1] == 1

    xp = x_prompt.reshape(nb * seq_len, d)
    xs = x_sample.reshape(ns, d)
    c_all = jnp.concatenate([c_sample, c_prompt], axis=0)
    row2 = lambda v: v.reshape(1, -1).astype(F32)

    conf_p, qkv_p, delta_p, conf_s, qkv_s, delta_s = [], [], [], [], [], []
    for l in range(depth):
        mod = _mod_call(c_all, w_ada[l].astype(F32), row2(b_ada[l]))
        mod_p = mod[ns:].reshape(nb, 1, N_MOD * d)
        mod_s = mod
        w_bf, w_tail = _split_w_in(w_in[l], d, dc, dqkv)
        n1w, n2w, fnw = row2(norm1_w[l]), row2(norm2_w[l]), row2(final_norm_w)
        alog_p = _lane_pad(a_log[l], N_HEADS)
        dtb_p = _lane_pad(dt_bias[l], N_HEADS)
        cw = gdn_conv_w[l].astype(F32)
        dww, dwb = conf_dw_w[l].astype(F32), row2(conf_dw_b[l])
        lnw, lnb = row2(conf_ln_w[l]), row2(conf_ln_b[l])
        gnw = row2(gdn_norm_w[l])
        wca, wgo, wo = w_conf_out[l].astype(BF16), w_gdn_out[l].astype(BF16), w_o[l].astype(BF16)
        w1, w2 = w_ff1[l].astype(BF16), w_ff2[l].astype(BF16)
        assert depth == 1

        glu, qkv, zg, bg, bgt, tail = _pre_prompt_call(
            xp, mod_p, n1w, w_bf, w_tail, cw, alog_p, dtb_p, seq_len, dc)
        o, s_fin = _gdn_prompt_call(qkv, bg, bgt, nb, seq_len)
        xp = _post_call(xp, mod_p, (glu, dww, dwb), lnw, lnb, o, zg, gnw, wca, wgo, wo, n2w, w1, w2, fnw,
                        seq_len=seq_len)
        conf_p.append(glu.reshape(nb, seq_len, dc)[:, seq_len - n_hist:])
        qkv_p.append(tail.reshape(nb, SUBLANES, dqkv)[:, SUBLANES - (n_taps - 1):])
        delta_p.append(s_fin)

        sconf = jnp.transpose(state_conf_conv[l].astype(F32), (1, 0, 2))
        sqkv = jnp.transpose(state_qkv_conv[l].astype(F32), (1, 0, 2))
        apre, qkv, zg, bg, nconf, nqkv = _pre_sample_call(
            xs, mod_s, n1w, w_bf, w_tail, cw, alog_p, dtb_p, sconf, sqkv, dww, dwb)
        o, s_new = _gdn_sample_call(qkv, bg, state_delta[l].astype(F32))
        xs = _post_call(xs, mod_s, (apre,), lnw, lnb, o, zg, gnw, wca, wgo, wo, n2w, w1, w2, fnw,
                        seq_len=1)
        conf_s.append(jnp.transpose(nconf, (1, 0, 2)))
        qkv_s.append(jnp.transpose(nqkv, (1, 0, 2)))
        delta_s.append(s_new)

    return (xp.reshape(nb, seq_len, d), xs.reshape(ns, 1, d),
            jnp.stack(conf_p), jnp.stack(qkv_p), jnp.stack(delta_p),
            jnp.stack(conf_s), jnp.stack(qkv_s), jnp.stack(delta_s))
```
